```python
import jax, jax.numpy as jnp
from jax import lax
import numpy as np

D_MODEL = 1024
BATCH = 8
SEQ = 2048
DEPTH = 1

CHUNK = 64
MEM_LEN = 256
CONV_DIM = D_MODEL
CONV_WIDTH = 3
SB_HEADS = 16
SB_HEAD_DIM = 64
SB_DIM = SB_HEADS * SB_HEAD_DIM
SB_BLOCK = 128
X_HEADS = 4
X_HEAD_DIM = 256
X_DIM = X_HEADS * X_HEAD_DIM
N_BRANCH = 3
D_FF = 4 * D_MODEL
EPS = 1e-6

IN_SPLITS = [CONV_DIM, CONV_DIM, CONV_DIM, SB_DIM, SB_DIM, SB_DIM, X_DIM, N_BRANCH * D_MODEL]
IN_COLS = int(sum(IN_SPLITS))

kernel_name = "hybrid_conv_stickbreaking_memxattn_block"


def rms_norm(x, g):
    xf = x.astype(jnp.float32)
    y = xf * lax.rsqrt(jnp.mean(xf * xf, axis=-1, keepdims=True) + EPS)
    return (y * g.astype(jnp.float32)).astype(x.dtype)


def causal_depthwise_conv(u, w):
    c = u.shape[-1]
    return lax.conv_general_dilated(
        u, w.astype(u.dtype)[:, None, :], window_strides=(1,),
        padding=[(CONV_WIDTH - 1, 0)],
        dimension_numbers=("NWC", "WIO", "NWC"),
        feature_group_count=c)


def stick_breaking_attention(q, k, v):
    seq = q.shape[2]
    scale = q.shape[-1] ** -0.5
    outs = []
    for start in range(0, seq, SB_BLOCK):
        end = start + SB_BLOCK
        qb = q[:, :, start:end]
        kb = k[:, :, :end]
        vb = v[:, :, :end]
        z = jnp.einsum("bhqd,bhkd->bhqk", qb, kb).astype(jnp.float32) * scale
        t_idx = start + jnp.arange(SB_BLOCK)[:, None]
        s_idx = jnp.arange(end)[None, :]
        past = s_idx < t_idx
        log_beta = jax.nn.log_sigmoid(z)
        log_1mb = jnp.where(past, jax.nn.log_sigmoid(-z), 0.0)
        between = lax.cumsum(log_1mb, axis=3, reverse=True) - log_1mb
        a = jnp.where(past, jnp.exp(log_beta + between), 0.0)
        outs.append(jnp.einsum("bhqk,bhkd->bhqd", a.astype(vb.dtype), vb))
    return jnp.concatenate(outs, axis=2)


def memory_cross_attention(xq, mem_n, w_mem_kv, q_norm_g, k_norm_g):
    b, s, _ = xq.shape
    m = mem_n.shape[1]
    q = xq.reshape(b, s, X_HEADS, X_HEAD_DIM)
    kv = mem_n @ w_mem_kv
    k, v = jnp.split(kv, 2, axis=-1)
    k = k.reshape(b, m, X_HEADS, X_HEAD_DIM)
    v = v.reshape(b, m, X_HEADS, X_HEAD_DIM)
    q = rms_norm(q, q_norm_g)
    k = rms_norm(k, k_norm_g)
    scores = jnp.einsum("bqhd,bkhd->bhqk", q, k).astype(jnp.float32) * (X_HEAD_DIM ** -0.5)
    p = jax.nn.softmax(scores, axis=-1).astype(v.dtype)
    o = jnp.einsum("bhqk,bkhd->bqhd", p, v)
    return o.reshape(b, s, X_DIM)


def setup_inputs(seed: int = 0) -> dict:
    key = jax.random.key(seed)
    ks = jax.random.split(key, 20)
    f32 = jnp.float32

    def nrm(k, shape, fan_in):
        return jax.random.normal(k, shape, f32) * (fan_in ** -0.5)

    def gain(k, shape):
        return 1.0 + 0.02 * jax.random.normal(k, shape, f32)

    L = DEPTH
    return {
        "x": jax.random.normal(ks[0], (BATCH, SEQ, D_MODEL), f32),
        "mem": jax.random.normal(ks[1], (BATCH, MEM_LEN, D_MODEL), f32),
        "g_mix": gain(ks[2], (L, D_MODEL)),
        "g_mem": gain(ks[3], (L, D_MODEL)),
        "w_in": nrm(ks[4], (L, D_MODEL, IN_COLS), D_MODEL),
        "conv_w": nrm(ks[5], (L, CONV_WIDTH, CONV_DIM), CONV_WIDTH),
        "w_conv_out": nrm(ks[6], (L, CONV_DIM, D_MODEL), CONV_DIM),
        "w_sb_out": nrm(ks[7], (L, SB_DIM, D_MODEL), SB_DIM),
        "q_norm_g": gain(ks[8], (L, X_HEAD_DIM)),
        "k_norm_g": gain(ks[9], (L, X_HEAD_DIM)),
        "w_mem_kv": nrm(ks[10], (L, D_MODEL, 2 * X_DIM), D_MODEL),
        "w_x_out": nrm(ks[11], (L, X_DIM, D_MODEL), X_DIM),
        "w_out": nrm(ks[12], (L, D_MODEL, D_MODEL), D_MODEL),
        "g_mlp": gain(ks[13], (L, D_MODEL)),
        "w_up": nrm(ks[14], (L, D_MODEL, D_FF), D_MODEL),
        "w_down": nrm(ks[15], (L, D_FF, D_MODEL), D_FF),
    }


def reference(x, mem, g_mix, g_mem, w_in, conv_w, w_conv_out, w_sb_out, q_norm_g,
              k_norm_g, w_mem_kv, w_x_out, w_out, g_mlp, w_up, w_down):
    b, s, d = x.shape
    split_idx = list(np.cumsum(IN_SPLITS)[:-1])
    for l in range(DEPTH):
        h = rms_norm(x, g_mix[l])
        proj = h @ w_in[l]
        c_h, c_b, c_c, sq, sk, sv, xq, gate_pre = jnp.split(proj, split_idx, axis=-1)

        y_conv = (c_b * causal_depthwise_conv(c_c * c_h, conv_w[l])) @ w_conv_out[l]

        to_heads = lambda t: t.reshape(b, s, SB_HEADS, SB_HEAD_DIM).transpose(0, 2, 1, 3)
        o_sb = stick_breaking_attention(to_heads(sq), to_heads(sk), to_heads(sv))
        y_sb = o_sb.transpose(0, 2, 1, 3).reshape(b, s, SB_DIM) @ w_sb_out[l]

        mem_n = rms_norm(mem, g_mem[l])
        y_x = memory_cross_attention(xq, mem_n, w_mem_kv[l], q_norm_g[l], k_norm_g[l]) @ w_x_out[l]

        gates = jax.nn.sigmoid(gate_pre.astype(jnp.float32)).astype(x.dtype)
        gates = gates.reshape(b, s, N_BRANCH, d)
        merged = gates[:, :, 0] * y_conv + gates[:, :, 1] * y_sb + gates[:, :, 2] * y_x
        x = x + merged @ w_out[l]

        h2 = rms_norm(x, g_mlp[l])
        x = x + jnp.square(jax.nn.relu(h2 @ w_up[l])) @ w_down[l]
    return x
```

```python
import functools

import jax
import jax.numpy as jnp
from jax import lax
from jax.experimental import pallas as pl
from jax.experimental.pallas import tpu as pltpu

D_MODEL = 1024
CONV_DIM = D_MODEL
CONV_WIDTH = 3
SB_HEADS = 16
SB_HEAD_DIM = 64
SB_DIM = SB_HEADS * SB_HEAD_DIM
X_HEADS = 4
X_HEAD_DIM = 256
X_DIM = X_HEADS * X_HEAD_DIM
N_BRANCH = 3
D_FF = 4 * D_MODEL
EPS = 1e-6

OFF_CH = 0
OFF_CB = OFF_CH + CONV_DIM
OFF_CC = OFF_CB + CONV_DIM
OFF_SQ = OFF_CC + CONV_DIM
OFF_SK = OFF_SQ + SB_DIM
OFF_SV = OFF_SK + SB_DIM
OFF_XQ = OFF_SV + SB_DIM
OFF_GATE = OFF_XQ + X_DIM
IN_COLS = OFF_GATE + N_BRANCH * D_MODEL

LANES = 128
SUBLANES = 8
VMEM_LIMIT = 56 * 1024 * 1024

MIX_TM = 256
POST_TM = 256
FF_CHUNK = 1024
SB_T = 128
HEADS_PER_PAIR = LANES // SB_HEAD_DIM

BF16 = jnp.bfloat16
F32 = jnp.float32


def _dot(a, b):
    return jnp.dot(a, b, preferred_element_type=F32)


def _rms(x, g):
    return x * lax.rsqrt(jnp.mean(x * x, axis=-1, keepdims=True) + EPS) * g


def _resident(shape):
    return pl.BlockSpec(shape, lambda *_: (0,) * len(shape), pipeline_mode=pl.Buffered(1))


def _mem_kv_kernel(mem_ref, gmem_ref, kg_ref, wkv_ref, kt_ref, v_ref):
    mn = _rms(mem_ref[...], gmem_ref[...]).astype(BF16)
    kv = _dot(mn, wkv_ref[...])
    for hh in range(X_HEADS):
        lo = hh * X_HEAD_DIM
        kn = _rms(kv[:, lo:lo + X_HEAD_DIM], kg_ref[...])
        kt_ref[lo:lo + X_HEAD_DIM, :] = kn.T.astype(BF16)
    v_ref[...] = kv[:, X_DIM:].astype(BF16)


def _mem_kv(mem, g_mem, k_norm_g, w_mem_kv):
    b, m, d = mem.shape
    return pl.pallas_call(
        _mem_kv_kernel,
        grid=(b,),
        in_specs=[
            pl.BlockSpec((None, m, d), lambda i: (i, 0, 0)),
            _resident((1, d)),
            _resident((1, X_HEAD_DIM)),
            _resident((d, 2 * X_DIM)),
        ],
        out_specs=[
            pl.BlockSpec((None, X_DIM, m), lambda i: (i, 0, 0)),
            pl.BlockSpec((None, m, X_DIM), lambda i: (i, 0, 0)),
        ],
        out_shape=[
            jax.ShapeDtypeStruct((b, X_DIM, m), BF16),
            jax.ShapeDtypeStruct((b, m, X_DIM), BF16),
        ],
        compiler_params=pltpu.CompilerParams(
            dimension_semantics=("arbitrary",), vmem_limit_bytes=VMEM_LIMIT),
        name="mem_kv",
    )(mem, g_mem, k_norm_g, w_mem_kv)


def _mixer_kernel(x_ref, kt_ref, v_ref, gmix_ref, qg_ref, convw_ref, win_ref, wco_ref, wxo_ref,
                  sq_ref, sk_ref, sv_ref, part_ref, gsb_ref, ubuf):
    tm = x_ref.shape[0]
    h = _rms(x_ref[...], gmix_ref[...]).astype(BF16)

    def proj(off, width):
        return _dot(h, win_ref[:, off:off + width])

    sq_ref[...] = proj(OFF_SQ, SB_DIM).astype(BF16)
    sk_ref[...] = proj(OFF_SK, SB_DIM).astype(BF16)
    sv_ref[...] = proj(OFF_SV, SB_DIM).astype(BF16)

    @pl.when(pl.program_id(1) == 0)
    def _():
        ubuf[0:SUBLANES, :] = jnp.zeros((SUBLANES, CONV_DIM), F32)

    u = proj(OFF_CC, CONV_DIM) * proj(OFF_CH, CONV_DIM)
    ubuf[SUBLANES:SUBLANES + tm, :] = u
    cw = convw_ref[...]
    conv = (cw[0:1, :] * ubuf[SUBLANES - 2:SUBLANES - 2 + tm, :]
            + cw[1:2, :] * ubuf[SUBLANES - 1:SUBLANES - 1 + tm, :]
            + cw[2:3, :] * u)
    ubuf[0:SUBLANES, :] = ubuf[tm:tm + SUBLANES, :]
    y_conv = _dot((proj(OFF_CB, CONV_DIM) * conv).astype(BF16), wco_ref[...])

    xq = proj(OFF_XQ, X_DIM)
    heads = []
    for hh in range(X_HEADS):
        lo = hh * X_HEAD_DIM
        qn = _rms(xq[:, lo:lo + X_HEAD_DIM], qg_ref[...]).astype(BF16)
        s = _dot(qn, kt_ref[lo:lo + X_HEAD_DIM, :]) * (X_HEAD_DIM ** -0.5)
        p = jnp.exp(s - jnp.max(s, axis=-1, keepdims=True))
        l = jnp.sum(p, axis=-1, keepdims=True)
        heads.append(_dot(p.astype(BF16), v_ref[:, lo:lo + X_HEAD_DIM]) / l)
    y_x = _dot(jnp.concatenate(heads, axis=-1).astype(BF16), wxo_ref[...])

    g_conv = jax.nn.sigmoid(proj(OFF_GATE, D_MODEL))
    g_x = jax.nn.sigmoid(proj(OFF_GATE + 2 * D_MODEL, D_MODEL))
    part_ref[...] = g_conv * y_conv + g_x * y_x
    gsb_ref[...] = jax.nn.sigmoid(proj(OFF_GATE + D_MODEL, D_MODEL))


def _mixer(x, kt, v, g_mix, q_norm_g, conv_w, w_in, w_conv_out, w_x_out):
    b, s, d = x.shape
    m = v.shape[1]
    tm = MIX_TM
    tok = lambda width: pl.BlockSpec((None, tm, width), lambda i, j: (i, j, 0))
    return pl.pallas_call(
        _mixer_kernel,
        grid=(b, s // tm),
        in_specs=[
            tok(d),
            pl.BlockSpec((None, X_DIM, m), lambda i, j: (i, 0, 0)),
            pl.BlockSpec((None, m, X_DIM), lambda i, j: (i, 0, 0)),
            _resident((1, d)),
            _resident((1, X_HEAD_DIM)),
            _resident((CONV_WIDTH, CONV_DIM)),
            _resident((d, IN_COLS)),
            _resident((CONV_DIM, d)),
            _resident((X_DIM, d)),
        ],
        out_specs=[tok(SB_DIM), tok(SB_DIM), tok(SB_DIM), tok(d), tok(d)],
        out_shape=[
            jax.ShapeDtypeStruct((b, s, SB_DIM), BF16),
            jax.ShapeDtypeStruct((b, s, SB_DIM), BF16),
            jax.ShapeDtypeStruct((b, s, SB_DIM), BF16),
            jax.ShapeDtypeStruct((b, s, d), F32),
            jax.ShapeDtypeStruct((b, s, d), F32),
        ],
        scratch_shapes=[pltpu.VMEM((tm + SUBLANES, CONV_DIM), F32)],
        compiler_params=pltpu.CompilerParams(
            dimension_semantics=("arbitrary", "arbitrary"), vmem_limit_bytes=VMEM_LIMIT),
        name="mixer",
    )(x, kt, v, g_mix, q_norm_g, conv_w, w_in, w_conv_out, w_x_out)


def _sb_kernel(q_ref, k_ref, v_ref, o_ref, kt2, v2):
    t = SB_T
    nblk = q_ref.shape[0] // t
    w2 = HEADS_PER_PAIR * t

    feat_row = lax.broadcasted_iota(jnp.int32, (LANES, t), 0)
    feat_lane = lax.broadcasted_iota(jnp.int32, (t, LANES), 1)
    for j in range(nblk):
        kb_t = k_ref[j * t:(j + 1) * t, :].astype(F32).T
        vb = v_ref[j * t:(j + 1) * t, :]
        kt2[j] = jnp.concatenate(
            [jnp.where(feat_row < SB_HEAD_DIM, kb_t, 0.0),
             jnp.where(feat_row >= SB_HEAD_DIM, kb_t, 0.0)], axis=1).astype(BF16)
        v2[j] = jnp.concatenate(
            [jnp.where(feat_lane < SB_HEAD_DIM, vb, jnp.zeros_like(vb)),
             jnp.where(feat_lane >= SB_HEAD_DIM, vb, jnp.zeros_like(vb))], axis=0)

    r = lax.broadcasted_iota(jnp.int32, (w2, w2), 0) % t
    c = lax.broadcasted_iota(jnp.int32, (w2, w2), 1)
    tri = jnp.where((c >= t) | (r > c), 1.0, 0.0).astype(BF16)

    row = lax.broadcasted_iota(jnp.int32, (t, w2), 0)
    col = lax.broadcasted_iota(jnp.int32, (t, w2), 1) % t
    past = col < row

    def tile(q, j, carry, acc, diag):
        z = _dot(q, kt2[j])
        log_1mb = -(jnp.maximum(z, 0.0) + jnp.log(1.0 + jnp.exp(-jnp.abs(z))))
        log_beta = z + log_1mb
        if diag:
            log_1mb = jnp.where(past, log_1mb, 0.0)
        hi = log_1mb.astype(BF16)
        lo = (log_1mb - hi.astype(F32)).astype(BF16)
        between, total = [], []
        for hh in range(HEADS_PER_PAIR):
            sl = slice(hh * t, (hh + 1) * t)
            res = _dot(jnp.concatenate([hi[:, sl], lo[:, sl]], axis=1), tri)
            between.append(res[:, :t])
            total.append(res[:, t:])
        between = jnp.concatenate(between, axis=1)
        total = jnp.concatenate(total, axis=1)
        a = jnp.exp(log_beta + between + carry)
        if diag:
            a = jnp.where(past, a, 0.0)
        acc = acc + _dot(a.astype(BF16), v2[j])
        return carry + total, acc

    def q_body(qi, _):
        q = q_ref[pl.ds(pl.multiple_of(qi * t, t), t), :] * (SB_HEAD_DIM ** -0.5)
        carry, acc = tile(q, qi, jnp.zeros((t, w2), F32), jnp.zeros((t, LANES), F32), True)

        def k_body(jj, st):
            return tile(q, qi - 1 - jj, st[0], st[1], False)

        _, acc = lax.fori_loop(0, qi, k_body, (carry, acc))
        o_ref[pl.ds(pl.multiple_of(qi * t, t), t), :] = acc.astype(o_ref.dtype)
        return 0

    lax.fori_loop(0, nblk, q_body, 0)


def _sb_attention(sq, sk, sv):
    b, s, _ = sq.shape
    npair = SB_DIM // LANES
    nblk = s // SB_T
    spec = pl.BlockSpec((None, s, LANES), lambda i, j: (i, 0, j))
    return pl.pallas_call(
        _sb_kernel,
        grid=(b, npair),
        in_specs=[spec, spec, spec],
        out_specs=spec,
        out_shape=jax.ShapeDtypeStruct((b, s, SB_DIM), BF16),
        scratch_shapes=[
            pltpu.VMEM((nblk, LANES, HEADS_PER_PAIR * SB_T), BF16),
            pltpu.VMEM((nblk, HEADS_PER_PAIR * SB_T, LANES), BF16),
        ],
        compiler_params=pltpu.CompilerParams(
            dimension_semantics=("arbitrary", "arbitrary"), vmem_limit_bytes=VMEM_LIMIT),
        name="sb_attention",
    )(sq, sk, sv)


def _post_kernel(x_ref, osb_ref, part_ref, gsb_ref, gmlp_ref, wsb_ref, wout_ref, wup_ref, wdn_ref,
                 out_ref):
    y_sb = _dot(osb_ref[...], wsb_ref[...])
    merged = part_ref[...] + gsb_ref[...] * y_sb
    x1 = x_ref[...] + _dot(merged.astype(BF16), wout_ref[...])
    h2 = _rms(x1, gmlp_ref[...]).astype(BF16)
    acc = x1
    for cidx in range(D_FF // FF_CHUNK):
        lo = cidx * FF_CHUNK
        up = jnp.maximum(_dot(h2, wup_ref[:, lo:lo + FF_CHUNK]), 0.0)
        acc = acc + _dot((up * up).astype(BF16), wdn_ref[lo:lo + FF_CHUNK, :])
    out_ref[...] = acc


def _post(x, o_sb, part, g_sb, g_mlp, w_sb_out, w_out, w_up, w_down):
    b, s, d = x.shape
    tm = POST_TM
    tok = lambda width: pl.BlockSpec((None, tm, width), lambda i, j: (i, j, 0))
    return pl.pallas_call(
        _post_kernel,
        grid=(b, s // tm),
        in_specs=[
            tok(d), tok(SB_DIM), tok(d), tok(d),
            _resident((1, d)),
            _resident((SB_DIM, d)),
            _resident((d, d)),
            _resident((d, D_FF)),
            _resident((D_FF, d)),
        ],
        out_specs=tok(d),
        out_shape=jax.ShapeDtypeStruct((b, s, d), F32),
        compiler_params=pltpu.CompilerParams(
            dimension_semantics=("arbitrary", "arbitrary"), vmem_limit_bytes=VMEM_LIMIT),
        name="post",
    )(x, o_sb, part, g_sb, g_mlp, w_sb_out, w_out, w_up, w_down)


def kernel(x, mem, g_mix, g_mem, w_in, conv_w, w_conv_out, w_sb_out, q_norm_g, k_norm_g,
           w_mem_kv, w_x_out, w_out, g_mlp, w_up, w_down):
    depth = w_in.shape[0]
    for l in range(depth):
        bf = lambda w: w[l].astype(BF16)
        row = lambda g: g[l][None, :]
        kt, v = _mem_kv(mem, row(g_mem), row(k_norm_g), bf(w_mem_kv))
        sq, sk, sv, part, g_sb = _mixer(
            x, kt, v, row(g_mix), row(q_norm_g), conv_w[l], bf(w_in), bf(w_conv_out), bf(w_x_out))
        o_sb = _sb_attention(sq, sk, sv)
        x = _post(x, o_sb, part, g_sb, row(g_mlp), bf(w_sb_out), bf(w_out), bf(w_up), bf(w_down))
    return x
```

```python
import functools

import jax
import jax.numpy as jnp
from jax import lax
from jax.experimental import pallas as pl
from jax.experimental.pallas import tpu as pltpu

D_MODEL = 1024
CONV_DIM = D_MODEL
CONV_WIDTH = 3
SB_HEADS = 16
SB_HEAD_DIM = 64
SB_DIM = SB_HEADS * SB_HEAD_DIM
X_HEADS = 4
X_HEAD_DIM = 256
X_DIM = X_HEADS * X_HEAD_DIM
N_BRANCH = 3
D_FF = 4 * D_MODEL
EPS = 1e-6

OFF_CH = 0
OFF_CB = OFF_CH + CONV_DIM
OFF_CC = OFF_CB + CONV_DIM
OFF_SQ = OFF_CC + CONV_DIM
OFF_SK = OFF_SQ + SB_DIM
OFF_SV = OFF_SK + SB_DIM
OFF_XQ = OFF_SV + SB_DIM
OFF_GATE = OFF_XQ + X_DIM
IN_COLS = OFF_GATE + N_BRANCH * D_MODEL

LANES = 128
SUBLANES = 8
VMEM_LIMIT = 56 * 1024 * 1024

MIX_TM = 256
POST_TM = 256
FF_CHUNK = 1024
SB_T = 128
SB_R = 4
HEADS_PER_PAIR = LANES // SB_HEAD_DIM
LOG2E = 1.4426950408889634
SB_K_SCALE = SB_HEAD_DIM ** -0.5 * LOG2E
SB_DEAD = -152.0
SB_Z_MAX = 126.0

BF16 = jnp.bfloat16
F32 = jnp.float32


def _dot(a, b):
    return jnp.dot(a, b, preferred_element_type=F32)


def _rms(x, g):
    return x * lax.rsqrt(jnp.mean(x * x, axis=-1, keepdims=True) + EPS) * g


def _resident(shape):
    return pl.BlockSpec(shape, lambda *_: (0,) * len(shape), pipeline_mode=pl.Buffered(1))


def _mem_kv_kernel(mem_ref, gmem_ref, kg_ref, wkv_ref, kt_ref, v_ref):
    mn = _rms(mem_ref[...], gmem_ref[...]).astype(BF16)
    kv = _dot(mn, wkv_ref[...])
    for hh in range(X_HEADS):
        lo = hh * X_HEAD_DIM
        kn = _rms(kv[:, lo:lo + X_HEAD_DIM], kg_ref[...])
        kt_ref[lo:lo + X_HEAD_DIM, :] = kn.T.astype(BF16)
    v_ref[...] = kv[:, X_DIM:].astype(BF16)


def _mem_kv(mem, g_mem, k_norm_g, w_mem_kv):
    b, m, d = mem.shape
    return pl.pallas_call(
        _mem_kv_kernel,
        grid=(b,),
        in_specs=[
            pl.BlockSpec((None, m, d), lambda i: (i, 0, 0)),
            _resident((1, d)),
            _resident((1, X_HEAD_DIM)),
            _resident((d, 2 * X_DIM)),
        ],
        out_specs=[
            pl.BlockSpec((None, X_DIM, m), lambda i: (i, 0, 0)),
            pl.BlockSpec((None, m, X_DIM), lambda i: (i, 0, 0)),
        ],
        out_shape=[
            jax.ShapeDtypeStruct((b, X_DIM, m), BF16),
            jax.ShapeDtypeStruct((b, m, X_DIM), BF16),
        ],
        compiler_params=pltpu.CompilerParams(
            dimension_semantics=("arbitrary",), vmem_limit_bytes=VMEM_LIMIT),
        name="mem_kv",
    )(mem, g_mem, k_norm_g, w_mem_kv)


def _mixer_kernel(x_ref, kt_ref, v_ref, gmix_ref, qg_ref, convw_ref, win_ref, wco_ref, wxo_ref,
                  sq_ref, sk_ref, sv_ref, part_ref, gsb_ref, ubuf):
    tm = x_ref.shape[0]
    h = _rms(x_ref[...], gmix_ref[...]).astype(BF16)

    def proj(off, width):
        return _dot(h, win_ref[:, off:off + width])

    sq_ref[...] = proj(OFF_SQ, SB_DIM).astype(BF16)
    sk_ref[...] = (proj(OFF_SK, SB_DIM) * SB_K_SCALE).astype(BF16)
    sv_ref[...] = proj(OFF_SV, SB_DIM).astype(BF16)

    @pl.when(pl.program_id(1) == 0)
    def _():
        ubuf[0:SUBLANES, :] = jnp.zeros((SUBLANES, CONV_DIM), F32)

    u = proj(OFF_CC, CONV_DIM) * proj(OFF_CH, CONV_DIM)
    ubuf[SUBLANES:SUBLANES + tm, :] = u
    cw = convw_ref[...]
    conv = (cw[0:1, :] * ubuf[SUBLANES - 2:SUBLANES - 2 + tm, :]
            + cw[1:2, :] * ubuf[SUBLANES - 1:SUBLANES - 1 + tm, :]
            + cw[2:3, :] * u)
    ubuf[0:SUBLANES, :] = ubuf[tm:tm + SUBLANES, :]
    y_conv = _dot((proj(OFF_CB, CONV_DIM) * conv).astype(BF16), wco_ref[...])

    xq = proj(OFF_XQ, X_DIM)
    heads = []
    for hh in range(X_HEADS):
        lo = hh * X_HEAD_DIM
        qn = _rms(xq[:, lo:lo + X_HEAD_DIM], qg_ref[...]).astype(BF16)
        s = _dot(qn, kt_ref[lo:lo + X_HEAD_DIM, :]) * (X_HEAD_DIM ** -0.5)
        p = jnp.exp(s - jnp.max(s, axis=-1, keepdims=True))
        l = jnp.sum(p, axis=-1, keepdims=True)
        heads.append(_dot(p.astype(BF16), v_ref[:, lo:lo + X_HEAD_DIM]) / l)
    y_x = _dot(jnp.concatenate(heads, axis=-1).astype(BF16), wxo_ref[...])

    g_conv = jax.nn.sigmoid(proj(OFF_GATE, D_MODEL))
    g_x = jax.nn.sigmoid(proj(OFF_GATE + 2 * D_MODEL, D_MODEL))
    part_ref[...] = g_conv * y_conv + g_x * y_x
    gsb_ref[...] = jax.nn.sigmoid(proj(OFF_GATE + D_MODEL, D_MODEL))


def _mixer(x, kt, v, g_mix, q_norm_g, conv_w, w_in, w_conv_out, w_x_out):
    b, s, d = x.shape
    m = v.shape[1]
    tm = MIX_TM
    tok = lambda width: pl.BlockSpec((None, tm, width), lambda i, j: (i, j, 0))
    return pl.pallas_call(
        _mixer_kernel,
        grid=(b, s // tm),
        in_specs=[
            tok(d),
            pl.BlockSpec((None, X_DIM, m), lambda i, j: (i, 0, 0)),
            pl.BlockSpec((None, m, X_DIM), lambda i, j: (i, 0, 0)),
            _resident((1, d)),
            _resident((1, X_HEAD_DIM)),
            _resident((CONV_WIDTH, CONV_DIM)),
            _resident((d, IN_COLS)),
            _resident((CONV_DIM, d)),
            _resident((X_DIM, d)),
        ],
        out_specs=[tok(SB_DIM), tok(SB_DIM), tok(SB_DIM), tok(d), tok(d)],
        out_shape=[
            jax.ShapeDtypeStruct((b, s, SB_DIM), BF16),
            jax.ShapeDtypeStruct((b, s, SB_DIM), BF16),
            jax.ShapeDtypeStruct((b, s, SB_DIM), BF16),
            jax.ShapeDtypeStruct((b, s, d), F32),
            jax.ShapeDtypeStruct((b, s, d), F32),
        ],
        scratch_shapes=[pltpu.VMEM((tm + SUBLANES, CONV_DIM), F32)],
        compiler_params=pltpu.CompilerParams(
            dimension_semantics=("arbitrary", "arbitrary"), vmem_limit_bytes=VMEM_LIMIT),
        name="mixer",
    )(x, kt, v, g_mix, q_norm_g, conv_w, w_in, w_conv_out, w_x_out)


def _sb_kernel(q_ref, k_ref, v_ref, o_ref, kt2, v2, carry_ref, acc_ref):
    t = SB_T
    tq = SB_R * t
    nblk = q_ref.shape[0] // t
    w2 = HEADS_PER_PAIR * t

    feat_row = lax.broadcasted_iota(jnp.int32, (LANES, t), 0)
    feat_lane = lax.broadcasted_iota(jnp.int32, (t, LANES), 1)
    for j in range(nblk):
        kb_t = k_ref[j * t:(j + 1) * t, :].astype(F32).T
        vb = v_ref[j * t:(j + 1) * t, :]
        kt2[j] = jnp.concatenate(
            [jnp.where(feat_row < SB_HEAD_DIM, kb_t, 0.0),
             jnp.where(feat_row >= SB_HEAD_DIM, kb_t, 0.0)], axis=1).astype(BF16)
        v2[j] = jnp.concatenate(
            [jnp.where(feat_lane < SB_HEAD_DIM, vb, jnp.zeros_like(vb)),
             jnp.where(feat_lane >= SB_HEAD_DIM, vb, jnp.zeros_like(vb))], axis=0)

    r = lax.broadcasted_iota(jnp.int32, (w2, w2), 0) % t
    c = lax.broadcasted_iota(jnp.int32, (w2, w2), 1)
    tri = jnp.where((c >= t) | (r > c), -1.0, 0.0).astype(BF16)

    def tile(q, j, carry, acc, diag):
        n = q.shape[0]
        z = _dot(q, kt2[j])
        sp = jnp.maximum(z, jnp.log(1.0 + jnp.exp2(jnp.minimum(z, SB_Z_MAX))) * LOG2E)
        log_beta = z - sp
        if diag:
            row = lax.broadcasted_iota(jnp.int32, (n, w2), 0)
            col = lax.broadcasted_iota(jnp.int32, (n, w2), 1) % t
            past = col < row
            sp = jnp.where(past, sp, 0.0)
        hi = sp.astype(BF16)
        lo = (sp - hi.astype(F32)).astype(BF16)
        between, total = [], []
        for hh in range(HEADS_PER_PAIR):
            sl = slice(hh * t, (hh + 1) * t)
            res = _dot(jnp.concatenate([hi[:, sl], lo[:, sl]], axis=1), tri)
            between.append(res[:, :t])
            total.append(res[:, t:])
        between = jnp.concatenate(between, axis=1)
        total = jnp.concatenate(total, axis=1)
        a = jnp.exp2(log_beta + between + carry)
        if diag:
            a = jnp.where(past, a, 0.0)
        acc = acc + _dot(a.astype(BF16), v2[j])
        return carry + total, acc

    def alive(carry):
        return jnp.max(carry) >= SB_DEAD

    def q_body(qi, _):
        base = pl.multiple_of(qi * tq, tq)
        carry = acc = None
        for jd in reversed(range(SB_R)):
            q = q_ref[pl.ds(base + jd * t, tq - jd * t), :]
            zc, za = jnp.zeros((t, w2), F32), jnp.zeros((t, LANES), F32)
            carry = zc if carry is None else jnp.concatenate([zc, carry], axis=0)
            acc = za if acc is None else jnp.concatenate([za, acc], axis=0)
            carry, acc = tile(q, SB_R * qi + jd, carry, acc, True)
        carry_ref[...] = carry
        acc_ref[...] = acc

        def k_cond(st):
            return jnp.logical_and(st[0] < SB_R * qi, st[1])

        def k_body(st):
            cnew, anew = tile(q_ref[pl.ds(base, tq), :], SB_R * qi - 1 - st[0],
                              carry_ref[...], acc_ref[...], False)
            carry_ref[...] = cnew
            acc_ref[...] = anew
            return st[0] + 1, alive(cnew)

        lax.while_loop(k_cond, k_body, (jnp.int32(0), alive(carry)))
        o_ref[pl.ds(base, tq), :] = acc_ref[...].astype(o_ref.dtype)
        return 0

    lax.fori_loop(0, nblk // SB_R, q_body, 0)


def _sb_attention(sq, sk, sv):
    b, s, _ = sq.shape
    npair = SB_DIM // LANES
    nblk = s // SB_T
    spec = pl.BlockSpec((None, s, LANES), lambda i, j: (i, 0, j))
    return pl.pallas_call(
        _sb_kernel,
        grid=(b, npair),
        in_specs=[spec, spec, spec],
        out_specs=spec,
        out_shape=jax.ShapeDtypeStruct((b, s, SB_DIM), BF16),
        scratch_shapes=[
            pltpu.VMEM((nblk, LANES, HEADS_PER_PAIR * SB_T), BF16),
            pltpu.VMEM((nblk, HEADS_PER_PAIR * SB_T, LANES), BF16),
            pltpu.VMEM((SB_R * SB_T, HEADS_PER_PAIR * SB_T), F32),
            pltpu.VMEM((SB_R * SB_T, LANES), F32),
        ],
        compiler_params=pltpu.CompilerParams(
            dimension_semantics=("arbitrary", "arbitrary"), vmem_limit_bytes=VMEM_LIMIT),
        name="sb_attention",
    )(sq, sk, sv)


def _post_kernel(x_ref, osb_ref, part_ref, gsb_ref, gmlp_ref, wsb_ref, wout_ref, wup_ref, wdn_ref,
                 out_ref):
    y_sb = _dot(osb_ref[...], wsb_ref[...])
    merged = part_ref[...] + gsb_ref[...] * y_sb
    x1 = x_ref[...] + _dot(merged.astype(BF16), wout_ref[...])
    h2 = _rms(x1, gmlp_ref[...]).astype(BF16)
    acc = x1
    for cidx in range(D_FF // FF_CHUNK):
        lo = cidx * FF_CHUNK
        up = jnp.maximum(_dot(h2, wup_ref[:, lo:lo + FF_CHUNK]), 0.0)
        acc = acc + _dot((up * up).astype(BF16), wdn_ref[lo:lo + FF_CHUNK, :])
    out_ref[...] = acc


def _post(x, o_sb, part, g_sb, g_mlp, w_sb_out, w_out, w_up, w_down):
    b, s, d = x.shape
    tm = POST_TM
    tok = lambda width: pl.BlockSpec((None, tm, width), lambda i, j: (i, j, 0))
    return pl.pallas_call(
        _post_kernel,
        grid=(b, s // tm),
        in_specs=[
            tok(d), tok(SB_DIM), tok(d), tok(d),
            _resident((1, d)),
            _resident((SB_DIM, d)),
            _resident((d, d)),
            _resident((d, D_FF)),
            _resident((D_FF, d)),
        ],
        out_specs=tok(d),
        out_shape=jax.ShapeDtypeStruct((b, s, d), F32),
        compiler_params=pltpu.CompilerParams(
            dimension_semantics=("arbitrary", "arbitrary"), vmem_limit_bytes=VMEM_LIMIT),
        name="post",
    )(x, o_sb, part, g_sb, g_mlp, w_sb_out, w_out, w_up, w_down)


def kernel(x, mem, g_mix, g_mem, w_in, conv_w, w_conv_out, w_sb_out, q_norm_g, k_norm_g,
           w_mem_kv, w_x_out, w_out, g_mlp, w_up, w_down):
    depth = w_in.shape[0]
    for l in range(depth):
        bf = lambda w: w[l].astype(BF16)
        row = lambda g: g[l][None, :]
        kt, v = _mem_kv(mem, row(g_mem), row(k_norm_g), bf(w_mem_kv))
        sq, sk, sv, part, g_sb = _mixer(
            x, kt, v, row(g_mix), row(q_norm_g), conv_w[l], bf(w_in), bf(w_conv_out), bf(w_x_out))
        o_sb = _sb_attention(sq, sk, sv)
        x = _post(x, o_sb, part, g_sb, row(g_mlp), bf(w_sb_out), bf(w_out), bf(w_up), bf(w_down))
    return x
```

```python
import functools

import jax
import jax.numpy as jnp
from jax import lax
from jax.experimental import pallas as pl
from jax.experimental.pallas import tpu as pltpu

D_MODEL = 1024
CONV_DIM = D_MODEL
CONV_WIDTH = 3
SB_HEADS = 16
SB_HEAD_DIM = 64
SB_DIM = SB_HEADS * SB_HEAD_DIM
X_HEADS = 4
X_HEAD_DIM = 256
X_DIM = X_HEADS * X_HEAD_DIM
N_BRANCH = 3
D_FF = 4 * D_MODEL
EPS = 1e-6

OFF_CH = 0
OFF_CB = OFF_CH + CONV_DIM
OFF_CC = OFF_CB + CONV_DIM
OFF_SQ = OFF_CC + CONV_DIM
OFF_SK = OFF_SQ + SB_DIM
OFF_SV = OFF_SK + SB_DIM
OFF_XQ = OFF_SV + SB_DIM
OFF_GATE = OFF_XQ + X_DIM
IN_COLS = OFF_GATE + N_BRANCH * D_MODEL

LANES = 128
SUBLANES = 8
VMEM_LIMIT = 56 * 1024 * 1024

MIX_TM = 256
POST_TM = 256
FF_CHUNK = 1024
SB_T = 128
SB_R = 2
SB_NEAR = 2
HEADS_PER_PAIR = LANES // SB_HEAD_DIM
LOG2E = 1.4426950408889634
SB_K_SCALE = SB_HEAD_DIM ** -0.5 * LOG2E
SB_DEAD = -152.0
SB_Z_MAX = 126.0

BF16 = jnp.bfloat16
F32 = jnp.float32


def _dot(a, b):
    return jnp.dot(a, b, preferred_element_type=F32)


def _rms(x, g):
    return x * lax.rsqrt(jnp.mean(x * x, axis=-1, keepdims=True) + EPS) * g


def _resident(shape):
    return pl.BlockSpec(shape, lambda *_: (0,) * len(shape), pipeline_mode=pl.Buffered(1))


def _mem_kv_kernel(mem_ref, gmem_ref, kg_ref, wkv_ref, kt_ref, v_ref):
    mn = _rms(mem_ref[...], gmem_ref[...]).astype(BF16)
    kv = _dot(mn, wkv_ref[...])
    for hh in range(X_HEADS):
        lo = hh * X_HEAD_DIM
        kn = _rms(kv[:, lo:lo + X_HEAD_DIM], kg_ref[...])
        kt_ref[lo:lo + X_HEAD_DIM, :] = kn.T.astype(BF16)
    v_ref[...] = kv[:, X_DIM:].astype(BF16)


def _mem_kv(mem, g_mem, k_norm_g, w_mem_kv):
    b, m, d = mem.shape
    return pl.pallas_call(
        _mem_kv_kernel,
        grid=(b,),
        in_specs=[
            pl.BlockSpec((None, m, d), lambda i: (i, 0, 0)),
            _resident((1, d)),
            _resident((1, X_HEAD_DIM)),
            _resident((d, 2 * X_DIM)),
        ],
        out_specs=[
            pl.BlockSpec((None, X_DIM, m), lambda i: (i, 0, 0)),
            pl.BlockSpec((None, m, X_DIM), lambda i: (i, 0, 0)),
        ],
        out_shape=[
            jax.ShapeDtypeStruct((b, X_DIM, m), BF16),
            jax.ShapeDtypeStruct((b, m, X_DIM), BF16),
        ],
        compiler_params=pltpu.CompilerParams(
            dimension_semantics=("arbitrary",), vmem_limit_bytes=VMEM_LIMIT),
        name="mem_kv",
    )(mem, g_mem, k_norm_g, w_mem_kv)


def _mixer_kernel(x_ref, kt_ref, v_ref, gmix_ref, qg_ref, convw_ref, win_ref, wco_ref, wxo_ref,
                  sq_ref, sk_ref, sv_ref, part_ref, gsb_ref, ubuf):
    tm = x_ref.shape[0]
    h = _rms(x_ref[...], gmix_ref[...]).astype(BF16)

    def proj(off, width):
        return _dot(h, win_ref[:, off:off + width])

    sq_ref[...] = proj(OFF_SQ, SB_DIM).astype(BF16)
    sk_ref[...] = (proj(OFF_SK, SB_DIM) * SB_K_SCALE).astype(BF16)
    sv_ref[...] = proj(OFF_SV, SB_DIM).astype(BF16)

    @pl.when(pl.program_id(1) == 0)
    def _():
        ubuf[0:SUBLANES, :] = jnp.zeros((SUBLANES, CONV_DIM), F32)

    u = proj(OFF_CC, CONV_DIM) * proj(OFF_CH, CONV_DIM)
    ubuf[SUBLANES:SUBLANES + tm, :] = u
    cw = convw_ref[...]
    conv = (cw[0:1, :] * ubuf[SUBLANES - 2:SUBLANES - 2 + tm, :]
            + cw[1:2, :] * ubuf[SUBLANES - 1:SUBLANES - 1 + tm, :]
            + cw[2:3, :] * u)
    ubuf[0:SUBLANES, :] = ubuf[tm:tm + SUBLANES, :]
    y_conv = _dot((proj(OFF_CB, CONV_DIM) * conv).astype(BF16), wco_ref[...])

    xq = proj(OFF_XQ, X_DIM)
    heads = []
    for hh in range(X_HEADS):
        lo = hh * X_HEAD_DIM
        qn = _rms(xq[:, lo:lo + X_HEAD_DIM], qg_ref[...]).astype(BF16)
        s = _dot(qn, kt_ref[lo:lo + X_HEAD_DIM, :]) * (X_HEAD_DIM ** -0.5)
        p = jnp.exp(s - jnp.max(s, axis=-1, keepdims=True))
        l = jnp.sum(p, axis=-1, keepdims=True)
        heads.append(_dot(p.astype(BF16), v_ref[:, lo:lo + X_HEAD_DIM]) / l)
    y_x = _dot(jnp.concatenate(heads, axis=-1).astype(BF16), wxo_ref[...])

    g_conv = jax.nn.sigmoid(proj(OFF_GATE, D_MODEL))
    g_x = jax.nn.sigmoid(proj(OFF_GATE + 2 * D_MODEL, D_MODEL))
    part_ref[...] = g_conv * y_conv + g_x * y_x
    gsb_ref[...] = jax.nn.sigmoid(proj(OFF_GATE + D_MODEL, D_MODEL))


def _mixer(x, kt, v, g_mix, q_norm_g, conv_w, w_in, w_conv_out, w_x_out):
    b, s, d = x.shape
    m = v.shape[1]
    tm = MIX_TM
    tok = lambda width: pl.BlockSpec((None, tm, width), lambda i, j: (i, j, 0))
    return pl.pallas_call(
        _mixer_kernel,
        grid=(b, s // tm),
        in_specs=[
            tok(d),
            pl.BlockSpec((None, X_DIM, m), lambda i, j: (i, 0, 0)),
            pl.BlockSpec((None, m, X_DIM), lambda i, j: (i, 0, 0)),
            _resident((1, d)),
            _resident((1, X_HEAD_DIM)),
            _resident((CONV_WIDTH, CONV_DIM)),
            _resident((d, IN_COLS)),
            _resident((CONV_DIM, d)),
            _resident((X_DIM, d)),
        ],
        out_specs=[tok(SB_DIM), tok(SB_DIM), tok(SB_DIM), tok(d), tok(d)],
        out_shape=[
            jax.ShapeDtypeStruct((b, s, SB_DIM), BF16),
            jax.ShapeDtypeStruct((b, s, SB_DIM), BF16),
            jax.ShapeDtypeStruct((b, s, SB_DIM), BF16),
            jax.ShapeDtypeStruct((b, s, d), F32),
            jax.ShapeDtypeStruct((b, s, d), F32),
        ],
        scratch_shapes=[pltpu.VMEM((tm + SUBLANES, CONV_DIM), F32)],
        compiler_params=pltpu.CompilerParams(
            dimension_semantics=("arbitrary", "arbitrary"), vmem_limit_bytes=VMEM_LIMIT),
        name="mixer",
    )(x, kt, v, g_mix, q_norm_g, conv_w, w_in, w_conv_out, w_x_out)


def _sb_kernel(q_ref, k_ref, v_ref, o_ref, kt2, v2, carry_ref, acc_ref):
    t = SB_T
    tq = SB_R * t
    nblk = q_ref.shape[0] // t
    w2 = HEADS_PER_PAIR * t

    feat_row = lax.broadcasted_iota(jnp.int32, (LANES, t), 0)
    feat_lane = lax.broadcasted_iota(jnp.int32, (t, LANES), 1)
    for j in range(nblk):
        kb_t = k_ref[j * t:(j + 1) * t, :].astype(F32).T
        vb = v_ref[j * t:(j + 1) * t, :]
        kt2[j] = jnp.concatenate(
            [jnp.where(feat_row < SB_HEAD_DIM, kb_t, 0.0),
             jnp.where(feat_row >= SB_HEAD_DIM, kb_t, 0.0)], axis=1).astype(BF16)
        v2[j] = jnp.concatenate(
            [jnp.where(feat_lane < SB_HEAD_DIM, vb, jnp.zeros_like(vb)),
             jnp.where(feat_lane >= SB_HEAD_DIM, vb, jnp.zeros_like(vb))], axis=0)

    r = lax.broadcasted_iota(jnp.int32, (w2, w2), 0) % t
    c = lax.broadcasted_iota(jnp.int32, (w2, w2), 1)
    tri = jnp.where((c >= t) | (r >= c), -1.0, 0.0).astype(BF16)

    row = lax.broadcasted_iota(jnp.int32, (t, w2), 0)
    col = lax.broadcasted_iota(jnp.int32, (t, w2), 1) % t
    past = col < row

    def mask_diag(x):
        head = jnp.where(past, x[:t], 0.0)
        return head if x.shape[0] == t else jnp.concatenate([head, x[t:]], axis=0)

    def run_tiles(base, tiles, carry, acc):
        zs, lhs = [], []
        for r0, j, diag in tiles:
            zs.append(_dot(q_ref[pl.ds(base + r0, tq - r0), :], kt2[j]))
        for z, (r0, j, diag) in zip(zs, tiles):
            sp = jnp.maximum(z, jnp.log(1.0 + jnp.exp2(jnp.minimum(z, SB_Z_MAX))) * LOG2E)
            if diag:
                sp = mask_diag(sp)
            hi = sp.astype(BF16)
            lo = (sp - hi.astype(F32)).astype(BF16)
            lhs.append([jnp.concatenate([hi[:, hh * t:(hh + 1) * t], lo[:, hh * t:(hh + 1) * t]], axis=1)
                        for hh in range(HEADS_PER_PAIR)])
        res = [[_dot(x, tri) for x in xs] for xs in lhs]
        avs = []
        for z, rs, (r0, j, diag) in zip(zs, res, tiles):
            below = jnp.concatenate([x[:, :t] for x in rs], axis=1)
            total = jnp.concatenate([x[:, t:] for x in rs], axis=1)
            a = jnp.exp2(z + below + carry[r0:])
            if diag:
                a = mask_diag(a)
            avs.append(a.astype(BF16))
            new = carry[r0:] + total
            carry = new if r0 == 0 else jnp.concatenate([carry[:r0], new], axis=0)
        for a, (r0, j, diag) in zip(avs, tiles):
            new = acc[r0:] + _dot(a, v2[j])
            acc = new if r0 == 0 else jnp.concatenate([acc[:r0], new], axis=0)
        return carry, acc

    def alive(carry):
        return jnp.max(carry) >= SB_DEAD

    def sweep(qi, n_near):
        base = qi * tq if isinstance(qi, int) else pl.multiple_of(qi * tq, tq)
        tiles = [(jd * t, SB_R * qi + jd, True) for jd in reversed(range(SB_R))]
        tiles += [(0, SB_R * qi - 1 - jb, False) for jb in range(n_near)]
        carry, acc = run_tiles(base, tiles, jnp.zeros((tq, w2), F32), jnp.zeros((tq, LANES), F32))
        carry_ref[...] = carry
        acc_ref[...] = acc

        def k_cond(st):
            return jnp.logical_and(st[0] < SB_R * qi, st[1])

        def k_body(st):
            cnew, anew = run_tiles(base, [(0, SB_R * qi - 1 - st[0], False)],
                                   carry_ref[...], acc_ref[...])
            carry_ref[...] = cnew
            acc_ref[...] = anew
            return st[0] + 1, alive(cnew)

        lax.while_loop(k_cond, k_body, (jnp.int32(n_near), alive(carry)))
        o_ref[pl.ds(base, tq), :] = acc_ref[...].astype(o_ref.dtype)

    sweep(0, 0)

    def q_body(qi, _):
        sweep(qi, SB_NEAR)
        return 0

    lax.fori_loop(1, nblk // SB_R, q_body, 0)


def _sb_attention(sq, sk, sv):
    b, s, _ = sq.shape
    npair = SB_DIM // LANES
    nblk = s // SB_T
    spec = pl.BlockSpec((None, s, LANES), lambda i, j: (i, 0, j))
    return pl.pallas_call(
        _sb_kernel,
        grid=(b, npair),
        in_specs=[spec, spec, spec],
        out_specs=spec,
        out_shape=jax.ShapeDtypeStruct((b, s, SB_DIM), BF16),
        scratch_shapes=[
            pltpu.VMEM((nblk, LANES, HEADS_PER_PAIR * SB_T), BF16),
            pltpu.VMEM((nblk, HEADS_PER_PAIR * SB_T, LANES), BF16),
            pltpu.VMEM((SB_R * SB_T, HEADS_PER_PAIR * SB_T), F32),
            pltpu.VMEM((SB_R * SB_T, LANES), F32),
        ],
        compiler_params=pltpu.CompilerParams(
            dimension_semantics=("arbitrary", "arbitrary"), vmem_limit_bytes=VMEM_LIMIT),
        name="sb_attention",
    )(sq, sk, sv)


def _post_kernel(x_ref, osb_ref, part_ref, gsb_ref, gmlp_ref, wsb_ref, wout_ref, wup_ref, wdn_ref,
                 out_ref):
    y_sb = _dot(osb_ref[...], wsb_ref[...])
    merged = part_ref[...] + gsb_ref[...] * y_sb
    x1 = x_ref[...] + _dot(merged.astype(BF16), wout_ref[...])
    h2 = _rms(x1, gmlp_ref[...]).astype(BF16)
    acc = x1
    for cidx in range(D_FF // FF_CHUNK):
        lo = cidx * FF_CHUNK
        up = jnp.maximum(_dot(h2, wup_ref[:, lo:lo + FF_CHUNK]), 0.0)
        acc = acc + _dot((up * up).astype(BF16), wdn_ref[lo:lo + FF_CHUNK, :])
    out_ref[...] = acc


def _post(x, o_sb, part, g_sb, g_mlp, w_sb_out, w_out, w_up, w_down):
    b, s, d = x.shape
    tm = POST_TM
    tok = lambda width: pl.BlockSpec((None, tm, width), lambda i, j: (i, j, 0))
    return pl.pallas_call(
        _post_kernel,
        grid=(b, s // tm),
        in_specs=[
            tok(d), tok(SB_DIM), tok(d), tok(d),
            _resident((1, d)),
            _resident((SB_DIM, d)),
            _resident((d, d)),
            _resident((d, D_FF)),
            _resident((D_FF, d)),
        ],
        out_specs=tok(d),
        out_shape=jax.ShapeDtypeStruct((b, s, d), F32),
        compiler_params=pltpu.CompilerParams(
            dimension_semantics=("arbitrary", "arbitrary"), vmem_limit_bytes=VMEM_LIMIT),
        name="post",
    )(x, o_sb, part, g_sb, g_mlp, w_sb_out, w_out, w_up, w_down)


def kernel(x, mem, g_mix, g_mem, w_in, conv_w, w_conv_out, w_sb_out, q_norm_g, k_norm_g,
           w_mem_kv, w_x_out, w_out, g_mlp, w_up, w_down):
    depth = w_in.shape[0]
    for l in range(depth):
        bf = lambda w: w[l].astype(BF16)
        row = lambda g: g[l][None, :]
        kt, v = _mem_kv(mem, row(g_mem), row(k_norm_g), bf(w_mem_kv))
        sq, sk, sv, part, g_sb = _mixer(
            x, kt, v, row(g_mix), row(q_norm_g), conv_w[l], bf(w_in), bf(w_conv_out), bf(w_x_out))
        o_sb = _sb_attention(sq, sk, sv)
        x = _post(x, o_sb, part, g_sb, row(g_mlp), bf(w_sb_out), bf(w_out), bf(w_up), bf(w_down))
    return x
```

```python
import functools

import jax
import jax.numpy as jnp
from jax import lax
from jax.experimental import pallas as pl
from jax.experimental.pallas import tpu as pltpu

D_MODEL = 1024
CONV_DIM = D_MODEL
CONV_WIDTH = 3
SB_HEADS = 16
SB_HEAD_DIM = 64
SB_DIM = SB_HEADS * SB_HEAD_DIM
X_HEADS = 4
X_HEAD_DIM = 256
X_DIM = X_HEADS * X_HEAD_DIM
N_BRANCH = 3
D_FF = 4 * D_MODEL
EPS = 1e-6

OFF_CH = 0
OFF_CB = OFF_CH + CONV_DIM
OFF_CC = OFF_CB + CONV_DIM
OFF_SQ = OFF_CC + CONV_DIM
OFF_SK = OFF_SQ + SB_DIM
OFF_SV = OFF_SK + SB_DIM
OFF_XQ = OFF_SV + SB_DIM
OFF_GATE = OFF_XQ + X_DIM
IN_COLS = OFF_GATE + N_BRANCH * D_MODEL

LANES = 128
SUBLANES = 8
VMEM_LIMIT = 56 * 1024 * 1024

MIX_TM = 256
POST_TM = 256
FF_CHUNK = 1024
SB_T = 128
SB_R = 2
SB_PAIRS = 2
HEADS_PER_PAIR = LANES // SB_HEAD_DIM
LOG2E = 1.4426950408889634
SB_K_SCALE = SB_HEAD_DIM ** -0.5 * LOG2E
SB_DEAD = -152.0
SB_Z_MAX = 126.0

BF16 = jnp.bfloat16
F32 = jnp.float32


def _dot(a, b):
    return jnp.dot(a, b, preferred_element_type=F32)


def _rms(x, g):
    return x * lax.rsqrt(jnp.mean(x * x, axis=-1, keepdims=True) + EPS) * g


def _resident(shape):
    return pl.BlockSpec(shape, lambda *_: (0,) * len(shape), pipeline_mode=pl.Buffered(1))


def _mem_kv_kernel(mem_ref, gmem_ref, kg_ref, wkv_ref, kt_ref, v_ref):
    mn = _rms(mem_ref[...], gmem_ref[...]).astype(BF16)
    kv = _dot(mn, wkv_ref[...])
    for hh in range(X_HEADS):
        lo = hh * X_HEAD_DIM
        kn = _rms(kv[:, lo:lo + X_HEAD_DIM], kg_ref[...])
        kt_ref[lo:lo + X_HEAD_DIM, :] = kn.T.astype(BF16)
    v_ref[...] = kv[:, X_DIM:].astype(BF16)


def _mem_kv(mem, g_mem, k_norm_g, w_mem_kv):
    b, m, d = mem.shape
    return pl.pallas_call(
        _mem_kv_kernel,
        grid=(b,),
        in_specs=[
            pl.BlockSpec((None, m, d), lambda i: (i, 0, 0)),
            _resident((1, d)),
            _resident((1, X_HEAD_DIM)),
            _resident((d, 2 * X_DIM)),
        ],
        out_specs=[
            pl.BlockSpec((None, X_DIM, m), lambda i: (i, 0, 0)),
            pl.BlockSpec((None, m, X_DIM), lambda i: (i, 0, 0)),
        ],
        out_shape=[
            jax.ShapeDtypeStruct((b, X_DIM, m), BF16),
            jax.ShapeDtypeStruct((b, m, X_DIM), BF16),
        ],
        compiler_params=pltpu.CompilerParams(
            dimension_semantics=("arbitrary",), vmem_limit_bytes=VMEM_LIMIT),
        name="mem_kv",
    )(mem, g_mem, k_norm_g, w_mem_kv)


def _mixer_kernel(x_ref, kt_ref, v_ref, gmix_ref, qg_ref, convw_ref, win_ref, wco_ref, wxo_ref,
                  sq_ref, sk_ref, sv_ref, part_ref, gsb_ref, ubuf):
    tm = x_ref.shape[0]
    h = _rms(x_ref[...], gmix_ref[...]).astype(BF16)

    def proj(off, width):
        return _dot(h, win_ref[:, off:off + width])

    sq_ref[...] = proj(OFF_SQ, SB_DIM).astype(BF16)
    sk_ref[...] = (proj(OFF_SK, SB_DIM) * SB_K_SCALE).astype(BF16)
    sv_ref[...] = proj(OFF_SV, SB_DIM).astype(BF16)

    @pl.when(pl.program_id(1) == 0)
    def _():
        ubuf[0:SUBLANES, :] = jnp.zeros((SUBLANES, CONV_DIM), F32)

    u = proj(OFF_CC, CONV_DIM) * proj(OFF_CH, CONV_DIM)
    ubuf[SUBLANES:SUBLANES + tm, :] = u
    cw = convw_ref[...]
    conv = (cw[0:1, :] * ubuf[SUBLANES - 2:SUBLANES - 2 + tm, :]
            + cw[1:2, :] * ubuf[SUBLANES - 1:SUBLANES - 1 + tm, :]
            + cw[2:3, :] * u)
    ubuf[0:SUBLANES, :] = ubuf[tm:tm + SUBLANES, :]
    y_conv = _dot((proj(OFF_CB, CONV_DIM) * conv).astype(BF16), wco_ref[...])

    xq = proj(OFF_XQ, X_DIM)
    heads = []
    for hh in range(X_HEADS):
        lo = hh * X_HEAD_DIM
        qn = _rms(xq[:, lo:lo + X_HEAD_DIM], qg_ref[...]).astype(BF16)
        s = _dot(qn, kt_ref[lo:lo + X_HEAD_DIM, :]) * (X_HEAD_DIM ** -0.5)
        p = jnp.exp(s - jnp.max(s, axis=-1, keepdims=True))
        l = jnp.sum(p, axis=-1, keepdims=True)
        heads.append(_dot(p.astype(BF16), v_ref[:, lo:lo + X_HEAD_DIM]) / l)
    y_x = _dot(jnp.concatenate(heads, axis=-1).astype(BF16), wxo_ref[...])

    g_conv = jax.nn.sigmoid(proj(OFF_GATE, D_MODEL))
    g_x = jax.nn.sigmoid(proj(OFF_GATE + 2 * D_MODEL, D_MODEL))
    part_ref[...] = g_conv * y_conv + g_x * y_x
    gsb_ref[...] = jax.nn.sigmoid(proj(OFF_GATE + D_MODEL, D_MODEL))


def _mixer(x, kt, v, g_mix, q_norm_g, conv_w, w_in, w_conv_out, w_x_out):
    b, s, d = x.shape
    m = v.shape[1]
    tm = MIX_TM
    tok = lambda width: pl.BlockSpec((None, tm, width), lambda i, j: (i, j, 0))
    return pl.pallas_call(
        _mixer_kernel,
        grid=(b, s // tm),
        in_specs=[
            tok(d),
            pl.BlockSpec((None, X_DIM, m), lambda i, j: (i, 0, 0)),
            pl.BlockSpec((None, m, X_DIM), lambda i, j: (i, 0, 0)),
            _resident((1, d)),
            _resident((1, X_HEAD_DIM)),
            _resident((CONV_WIDTH, CONV_DIM)),
            _resident((d, IN_COLS)),
            _resident((CONV_DIM, d)),
            _resident((X_DIM, d)),
        ],
        out_specs=[tok(SB_DIM), tok(SB_DIM), tok(SB_DIM), tok(d), tok(d)],
        out_shape=[
            jax.ShapeDtypeStruct((b, s, SB_DIM), BF16),
            jax.ShapeDtypeStruct((b, s, SB_DIM), BF16),
            jax.ShapeDtypeStruct((b, s, SB_DIM), BF16),
            jax.ShapeDtypeStruct((b, s, d), F32),
            jax.ShapeDtypeStruct((b, s, d), F32),
        ],
        scratch_shapes=[pltpu.VMEM((tm + SUBLANES, CONV_DIM), F32)],
        compiler_params=pltpu.CompilerParams(
            dimension_semantics=("arbitrary", "arbitrary"), vmem_limit_bytes=VMEM_LIMIT),
        name="mixer",
    )(x, kt, v, g_mix, q_norm_g, conv_w, w_in, w_conv_out, w_x_out)


def _sb_kernel(q_ref, k_ref, v_ref, o_ref, kt2, v2, tri_ref, carry_ref, acc_ref):
    t = SB_T
    tq = SB_R * t
    nblk = q_ref.shape[0] // t
    w2 = HEADS_PER_PAIR * t
    pairs = range(SB_PAIRS)

    def lanes(p):
        return slice(p * LANES, (p + 1) * LANES)

    feat_row = lax.broadcasted_iota(jnp.int32, (LANES, t), 0)
    feat_lane = lax.broadcasted_iota(jnp.int32, (t, LANES), 1)
    for p in pairs:
        for j in range(nblk):
            kb_t = k_ref[j * t:(j + 1) * t, lanes(p)].astype(F32).T
            vb = v_ref[j * t:(j + 1) * t, lanes(p)]
            kt2[p, j] = jnp.concatenate(
                [jnp.where(feat_row < SB_HEAD_DIM, kb_t, 0.0),
                 jnp.where(feat_row >= SB_HEAD_DIM, kb_t, 0.0)], axis=1).astype(BF16)
            v2[p, j] = jnp.concatenate(
                [jnp.where(feat_lane < SB_HEAD_DIM, vb, jnp.zeros_like(vb)),
                 jnp.where(feat_lane >= SB_HEAD_DIM, vb, jnp.zeros_like(vb))], axis=0)

    r = lax.broadcasted_iota(jnp.int32, (w2, w2), 0) % t
    c = lax.broadcasted_iota(jnp.int32, (w2, w2), 1)
    tri_ref[...] = jnp.where((c >= t) | (r >= c), -1.0, 0.0).astype(BF16)

    row = lax.broadcasted_iota(jnp.int32, (t, w2), 0)
    col = lax.broadcasted_iota(jnp.int32, (t, w2), 1) % t
    past = col < row

    def mask_diag(x):
        head = jnp.where(past, x[:t], 0.0)
        return head if x.shape[0] == t else jnp.concatenate([head, x[t:]], axis=0)

    def run_tiles(base, tiles, carry, acc):
        carry, acc = list(carry), list(acc)

        def put(full, r0, r1, new):
            parts = ([full[:r0]] if r0 > 0 else []) + [new] + ([full[r1:]] if r1 < full.shape[0] else [])
            return new if len(parts) == 1 else jnp.concatenate(parts, axis=0)

        zs, lhs = [], []
        for p, r0, r1, j, diag in tiles:
            zs.append(_dot(q_ref[pl.ds(base + r0, r1 - r0), lanes(p)], kt2[p, j]))
        for z, (p, r0, r1, j, diag) in zip(zs, tiles):
            sp = jnp.maximum(z, jnp.log(1.0 + jnp.exp2(jnp.minimum(z, SB_Z_MAX))) * LOG2E)
            if diag:
                sp = mask_diag(sp)
            hi = sp.astype(BF16)
            lo = (sp - hi.astype(F32)).astype(BF16)
            lhs.append([jnp.concatenate([hi[:, hh * t:(hh + 1) * t], lo[:, hh * t:(hh + 1) * t]], axis=1)
                        for hh in range(HEADS_PER_PAIR)])
        res = [[_dot(x, tri_ref[...]) for x in xs] for xs in lhs]
        avs = []
        for z, rs, (p, r0, r1, j, diag) in zip(zs, res, tiles):
            below = jnp.concatenate([x[:, :t] for x in rs], axis=1)
            total = jnp.concatenate([x[:, t:] for x in rs], axis=1)
            a = jnp.exp2(z + below + carry[p][r0:r1])
            if diag:
                a = mask_diag(a)
            avs.append(a.astype(BF16))
            carry[p] = put(carry[p], r0, r1, carry[p][r0:r1] + total)
        for a, (p, r0, r1, j, diag) in zip(avs, tiles):
            acc[p] = put(acc[p], r0, r1, acc[p][r0:r1] + _dot(a, v2[p, j]))
        return carry, acc

    def alive(carry):
        return functools.reduce(jnp.maximum, [jnp.max(x) for x in carry]) >= SB_DEAD

    def sweep(qi, first):
        base = qi * tq if first else pl.multiple_of(qi * tq, tq)
        ja = 2 * qi
        steps = [(t, tq, ja + 1, True), (0, tq, ja, True)]
        if not first:
            steps += [(0, tq, ja - 1, False), (0, t, ja - 2, False)]
        carry, acc = run_tiles(base, [(p,) + st for st in steps for p in pairs],
                               [jnp.zeros((tq, w2), F32) for _ in pairs],
                               [jnp.zeros((tq, LANES), F32) for _ in pairs])
        for p in pairs:
            carry_ref[p] = carry[p]
            acc_ref[p] = acc[p]
        if first:
            for p in pairs:
                o_ref[pl.ds(base, tq), lanes(p)] = acc[p].astype(o_ref.dtype)
            return

        rest = ((0, t, ja - 3), (t, tq, ja - 2))
        alive0 = [alive([x[r0:r1] for x in carry]) for r0, r1, _ in rest]
        for (r0, r1, j_next), go in zip(rest, alive0):
            def k_cond(st):
                return jnp.logical_and(st[0] >= 0, st[1])

            def k_body(st, r0=r0, r1=r1):
                cnew, anew = run_tiles(base, [(p, r0, r1, st[0], False) for p in pairs],
                                       [carry_ref[p] for p in pairs], [acc_ref[p] for p in pairs])
                for p in pairs:
                    carry_ref[p] = cnew[p]
                    acc_ref[p] = anew[p]
                return st[0] - 1, alive([x[r0:r1] for x in cnew])

            lax.while_loop(k_cond, k_body, (j_next, go))
        for p in pairs:
            o_ref[pl.ds(base, tq), lanes(p)] = acc_ref[p].astype(o_ref.dtype)

    sweep(0, True)

    def q_body(qi, _):
        sweep(qi, False)
        return 0

    lax.fori_loop(1, nblk // SB_R, q_body, 0)


def _sb_attention(sq, sk, sv):
    b, s, _ = sq.shape
    width = SB_PAIRS * LANES
    nblk = s // SB_T
    spec = pl.BlockSpec((None, s, width), lambda i, j: (i, 0, j))
    return pl.pallas_call(
        _sb_kernel,
        grid=(b, SB_DIM // width),
        in_specs=[spec, spec, spec],
        out_specs=spec,
        out_shape=jax.ShapeDtypeStruct((b, s, SB_DIM), BF16),
        scratch_shapes=[
            pltpu.VMEM((SB_PAIRS, nblk, LANES, HEADS_PER_PAIR * SB_T), BF16),
            pltpu.VMEM((SB_PAIRS, nblk, HEADS_PER_PAIR * SB_T, LANES), BF16),
            pltpu.VMEM((HEADS_PER_PAIR * SB_T, HEADS_PER_PAIR * SB_T), BF16),
            pltpu.VMEM((SB_PAIRS, SB_R * SB_T, HEADS_PER_PAIR * SB_T), F32),
            pltpu.VMEM((SB_PAIRS, SB_R * SB_T, LANES), F32),
        ],
        compiler_params=pltpu.CompilerParams(
            dimension_semantics=("arbitrary", "arbitrary"), vmem_limit_bytes=VMEM_LIMIT),
        name="sb_attention",
    )(sq, sk, sv)


def _post_kernel(x_ref, osb_ref, part_ref, gsb_ref, gmlp_ref, wsb_ref, wout_ref, wup_ref, wdn_ref,
                 out_ref):
    y_sb = _dot(osb_ref[...], wsb_ref[...])
    merged = part_ref[...] + gsb_ref[...] * y_sb
    x1 = x_ref[...] + _dot(merged.astype(BF16), wout_ref[...])
    h2 = _rms(x1, gmlp_ref[...]).astype(BF16)
    acc = x1
    for cidx in range(D_FF // FF_CHUNK):
        lo = cidx * FF_CHUNK
        up = jnp.maximum(_dot(h2, wup_ref[:, lo:lo + FF_CHUNK]), 0.0)
        acc = acc + _dot((up * up).astype(BF16), wdn_ref[lo:lo + FF_CHUNK, :])
    out_ref[...] = acc


def _post(x, o_sb, part, g_sb, g_mlp, w_sb_out, w_out, w_up, w_down):
    b, s, d = x.shape
    tm = POST_TM
    tok = lambda width: pl.BlockSpec((None, tm, width), lambda i, j: (i, j, 0))
    return pl.pallas_call(
        _post_kernel,
        grid=(b, s // tm),
        in_specs=[
            tok(d), tok(SB_DIM), tok(d), tok(d),
            _resident((1, d)),
            _resident((SB_DIM, d)),
            _resident((d, d)),
            _resident((d, D_FF)),
            _resident((D_FF, d)),
        ],
        out_specs=tok(d),
        out_shape=jax.ShapeDtypeStruct((b, s, d), F32),
        compiler_params=pltpu.CompilerParams(
            dimension_semantics=("arbitrary", "arbitrary"), vmem_limit_bytes=VMEM_LIMIT),
        name="post",
    )(x, o_sb, part, g_sb, g_mlp, w_sb_out, w_out, w_up, w_down)


def kernel(x, mem, g_mix, g_mem, w_in, conv_w, w_conv_out, w_sb_out, q_norm_g, k_norm_g,
           w_mem_kv, w_x_out, w_out, g_mlp, w_up, w_down):
    depth = w_in.shape[0]
    for l in range(depth):
        bf = lambda w: w[l].astype(BF16)
        row = lambda g: g[l][None, :]
        kt, v = _mem_kv(mem, row(g_mem), row(k_norm_g), bf(w_mem_kv))
        sq, sk, sv, part, g_sb = _mixer(
            x, kt, v, row(g_mix), row(q_norm_g), conv_w[l], bf(w_in), bf(w_conv_out), bf(w_x_out))
        o_sb = _sb_attention(sq, sk, sv)
        x = _post(x, o_sb, part, g_sb, row(g_mlp), bf(w_sb_out), bf(w_out), bf(w_up), bf(w_down))
    return x
```

```python
import functools

import jax
import jax.numpy as jnp
from jax import lax
from jax.experimental import pallas as pl
from jax.experimental.pallas import tpu as pltpu

D_MODEL = 1024
CONV_DIM = D_MODEL
CONV_WIDTH = 3
SB_HEADS = 16
SB_HEAD_DIM = 64
SB_DIM = SB_HEADS * SB_HEAD_DIM
X_HEADS = 4
X_HEAD_DIM = 256
X_DIM = X_HEADS * X_HEAD_DIM
N_BRANCH = 3
D_FF = 4 * D_MODEL
EPS = 1e-6

OFF_CH = 0
OFF_CB = OFF_CH + CONV_DIM
OFF_CC = OFF_CB + CONV_DIM
OFF_SQ = OFF_CC + CONV_DIM
OFF_SK = OFF_SQ + SB_DIM
OFF_SV = OFF_SK + SB_DIM
OFF_XQ = OFF_SV + SB_DIM
OFF_GATE = OFF_XQ + X_DIM
IN_COLS = OFF_GATE + N_BRANCH * D_MODEL

LANES = 128
SUBLANES = 8
VMEM_LIMIT = 60 * 1024 * 1024

SB_T = 128
SB_R = 2
MIX_TM = SB_R * SB_T
POST_TM = 512
FF_CHUNK = 1024
SB_NPAIR = SB_DIM // LANES
SB_GROUP = 4
SB_PAD = 2
HEADS_PER_PAIR = LANES // SB_HEAD_DIM
LOG2E = 1.4426950408889634
SB_K_SCALE = SB_HEAD_DIM ** -0.5 * LOG2E
SB_DEAD = -152.0
SB_Z_MAX = 126.0

BF16 = jnp.bfloat16
F32 = jnp.float32


def _dot(a, b):
    return jnp.dot(a, b, preferred_element_type=F32)


def _rms(x, g):
    return x * lax.rsqrt(jnp.mean(x * x, axis=-1, keepdims=True) + EPS) * g


def _resident(shape):
    return pl.BlockSpec(shape, lambda *_: (0,) * len(shape), pipeline_mode=pl.Buffered(1))


def _lanes(p):
    return slice(p * LANES, (p + 1) * LANES)


def _mem_kv_kernel(mem_ref, gmem_ref, kg_ref, wkv_ref, kt_ref, v_ref):
    mn = _rms(mem_ref[...], gmem_ref[...]).astype(BF16)
    kv = _dot(mn, wkv_ref[...])
    for hh in range(X_HEADS):
        lo = hh * X_HEAD_DIM
        kn = _rms(kv[:, lo:lo + X_HEAD_DIM], kg_ref[...])
        kt_ref[lo:lo + X_HEAD_DIM, :] = kn.T.astype(BF16)
    v_ref[...] = kv[:, X_DIM:].astype(BF16)


def _mem_kv(mem, g_mem, k_norm_g, w_mem_kv):
    b, m, d = mem.shape
    return pl.pallas_call(
        _mem_kv_kernel,
        grid=(b,),
        in_specs=[
            pl.BlockSpec((None, m, d), lambda i: (i, 0, 0)),
            _resident((1, d)),
            _resident((1, X_HEAD_DIM)),
            _resident((d, 2 * X_DIM)),
        ],
        out_specs=[
            pl.BlockSpec((None, X_DIM, m), lambda i: (i, 0, 0)),
            pl.BlockSpec((None, m, X_DIM), lambda i: (i, 0, 0)),
        ],
        out_shape=[
            jax.ShapeDtypeStruct((b, X_DIM, m), BF16),
            jax.ShapeDtypeStruct((b, m, X_DIM), BF16),
        ],
        compiler_params=pltpu.CompilerParams(
            dimension_semantics=("arbitrary",), vmem_limit_bytes=VMEM_LIMIT),
        name="mem_kv",
    )(mem, g_mem, k_norm_g, w_mem_kv)


def _mixer_kernel(x_ref, kt_ref, v_ref, gmix_ref, qg_ref, convw_ref, win_ref, wco_ref, wxo_ref,
                  osb_ref, part_ref, gsb_ref,
                  ubuf, q_scr, kth, vh, tri_ref, carry_ref, acc_ref):
    t = SB_T
    tq = MIX_TM
    w2 = HEADS_PER_PAIR * t
    tile = pl.program_id(1)
    ja = 2 * tile
    h = _rms(x_ref[...], gmix_ref[...]).astype(BF16)

    def proj(off, width):
        return _dot(h, win_ref[:, off:off + width])

    @pl.when(tile == 0)
    def _():
        ubuf[0:SUBLANES, :] = jnp.zeros((SUBLANES, CONV_DIM), F32)
        for p in range(SB_NPAIR):
            kth[p, 0:SB_PAD] = jnp.zeros((SB_PAD, LANES, t), BF16)
            vh[p, 0:SB_PAD] = jnp.zeros((SB_PAD, t, LANES), BF16)

    q_scr[...] = proj(OFF_SQ, SB_DIM).astype(BF16)
    k_new = proj(OFF_SK, SB_DIM) * SB_K_SCALE
    v_new = proj(OFF_SV, SB_DIM).astype(BF16)
    for p in range(SB_NPAIR):
        for blk in range(SB_R):
            rows = slice(blk * t, (blk + 1) * t)
            kth[p, ja + SB_PAD + blk] = k_new[rows, _lanes(p)].T.astype(BF16)
            vh[p, ja + SB_PAD + blk] = v_new[rows, _lanes(p)]

    r = lax.broadcasted_iota(jnp.int32, (w2, w2), 0) % t
    c = lax.broadcasted_iota(jnp.int32, (w2, w2), 1)
    tri_ref[...] = jnp.where((c >= t) | (r >= c), -1.0, 0.0).astype(BF16)

    row = lax.broadcasted_iota(jnp.int32, (t, w2), 0)
    col = lax.broadcasted_iota(jnp.int32, (t, w2), 1) % t
    past = col < row

    feat_row = lax.broadcasted_iota(jnp.int32, (LANES, t), 0)
    feat_lane = lax.broadcasted_iota(jnp.int32, (t, LANES), 1)
    head_rows = [jnp.where(feat_row // SB_HEAD_DIM == hh, 1.0, 0.0).astype(BF16)
                 for hh in range(HEADS_PER_PAIR)]
    head_lanes = [jnp.where(feat_lane // SB_HEAD_DIM == hh, 1.0, 0.0).astype(BF16)
                  for hh in range(HEADS_PER_PAIR)]

    def mask_diag(x):
        head = jnp.where(past, x[:t], 0.0)
        return head if x.shape[0] == t else jnp.concatenate([head, x[t:]], axis=0)

    def put(full, r0, r1, new):
        parts = ([full[:r0]] if r0 > 0 else []) + [new] + ([full[r1:]] if r1 < full.shape[0] else [])
        return new if len(parts) == 1 else jnp.concatenate(parts, axis=0)

    def tiles_stages(tiles, carry, acc):
        carry, acc = dict(carry), dict(acc)
        zs, lhs = [], []
        for p, r0, r1, j, diag in tiles:
            kt = kth[p, j + SB_PAD]
            kt2 = jnp.concatenate([kt * m for m in head_rows], axis=1)
            zs.append(_dot(q_scr[r0:r1, _lanes(p)], kt2))
        yield None
        for z, (p, r0, r1, j, diag) in zip(zs, tiles):
            sp = jnp.maximum(z, jnp.log(1.0 + jnp.exp2(jnp.minimum(z, SB_Z_MAX))) * LOG2E)
            if diag:
                sp = mask_diag(sp)
            hi = sp.astype(BF16)
            lo = (sp - hi.astype(F32)).astype(BF16)
            lhs.append([jnp.concatenate([hi[:, hh * t:(hh + 1) * t], lo[:, hh * t:(hh + 1) * t]], axis=1)
                        for hh in range(HEADS_PER_PAIR)])
        res = [[_dot(x, tri_ref[...]) for x in xs] for xs in lhs]
        yield None
        avs = []
        for z, rs, (p, r0, r1, j, diag) in zip(zs, res, tiles):
            below = jnp.concatenate([x[:, :t] for x in rs], axis=1)
            total = jnp.concatenate([x[:, t:] for x in rs], axis=1)
            a = jnp.exp2(z + below + carry[p][r0:r1])
            if diag:
                a = mask_diag(a)
            avs.append(a.astype(BF16))
            carry[p] = put(carry[p], r0, r1, carry[p][r0:r1] + total)
        for a, (p, r0, r1, j, diag) in zip(avs, tiles):
            vb = vh[p, j + SB_PAD]
            v2 = jnp.concatenate([vb * m for m in head_lanes], axis=0)
            acc[p] = put(acc[p], r0, r1, acc[p][r0:r1] + _dot(a, v2))
        yield carry, acc

    steps = [(t, tq, ja + 1, True), (0, tq, ja, True), (0, tq, ja - 1, False), (0, t, ja - 2, False)]
    groups = [range(g, g + SB_GROUP) for g in range(0, SB_NPAIR, SB_GROUP)]
    sweeps = [tiles_stages([(p,) + st for st in steps for p in grp],
                           {p: jnp.zeros((tq, w2), F32) for p in grp},
                           {p: jnp.zeros((tq, LANES), F32) for p in grp}) for grp in groups]

    next(sweeps[0])

    u = proj(OFF_CC, CONV_DIM) * proj(OFF_CH, CONV_DIM)
    ubuf[SUBLANES:SUBLANES + tq, :] = u
    cw = convw_ref[...]
    conv = (cw[0:1, :] * ubuf[SUBLANES - 2:SUBLANES - 2 + tq, :]
            + cw[1:2, :] * ubuf[SUBLANES - 1:SUBLANES - 1 + tq, :]
            + cw[2:3, :] * u)
    ubuf[0:SUBLANES, :] = ubuf[tq:tq + SUBLANES, :]
    conv_in = (proj(OFF_CB, CONV_DIM) * conv).astype(BF16)

    next(sweeps[0])
    y_conv = _dot(conv_in, wco_ref[...])
    xq = proj(OFF_XQ, X_DIM)
    state = [next(sweeps[0])]

    next(sweeps[1])

    heads = []
    for hh in range(X_HEADS):
        lo = hh * X_HEAD_DIM
        qn = _rms(xq[:, lo:lo + X_HEAD_DIM], qg_ref[...]).astype(BF16)
        s = _dot(qn, kt_ref[lo:lo + X_HEAD_DIM, :]) * (X_HEAD_DIM ** -0.5)
        pr = jnp.exp(s - jnp.max(s, axis=-1, keepdims=True))
        l = jnp.sum(pr, axis=-1, keepdims=True)
        heads.append(_dot(pr.astype(BF16), v_ref[:, lo:lo + X_HEAD_DIM]) / l)
    y_x = _dot(jnp.concatenate(heads, axis=-1).astype(BF16), wxo_ref[...])
    g_conv = jax.nn.sigmoid(proj(OFF_GATE, D_MODEL))

    next(sweeps[1])
    g_x = jax.nn.sigmoid(proj(OFF_GATE + 2 * D_MODEL, D_MODEL))
    part_ref[...] = (g_conv * y_conv + g_x * y_x).astype(part_ref.dtype)
    gsb_ref[...] = jax.nn.sigmoid(proj(OFF_GATE + D_MODEL, D_MODEL)).astype(gsb_ref.dtype)
    state.append(next(sweeps[1]))

    carry = {p: c for cs, _ in state for p, c in cs.items()}
    acc = {p: a for _, accs in state for p, a in accs.items()}
    for p in range(SB_NPAIR):
        carry_ref[p] = carry[p]
        acc_ref[p] = acc[p]

    def alive(rows):
        return functools.reduce(jnp.maximum, [jnp.max(x) for x in rows]) >= SB_DEAD

    rest = ((0, t, ja - 3), (t, tq, ja - 2))
    alive0 = [alive([carry[p][r0:r1] for p in range(SB_NPAIR)]) for r0, r1, _ in rest]
    for (r0, r1, j_next), go in zip(rest, alive0):
        def k_cond(st):
            return jnp.logical_and(st[0] >= 0, st[1])

        def k_body(st, r0=r0, r1=r1):
            *_, (cnew, anew) = tiles_stages([(p, r0, r1, st[0], False) for p in range(SB_NPAIR)],
                                            {p: carry_ref[p] for p in range(SB_NPAIR)},
                                            {p: acc_ref[p] for p in range(SB_NPAIR)})
            for p in range(SB_NPAIR):
                carry_ref[p] = cnew[p]
                acc_ref[p] = anew[p]
            return st[0] - 1, alive([cnew[p][r0:r1] for p in range(SB_NPAIR)])

        lax.while_loop(k_cond, k_body, (j_next, go))
    for p in range(SB_NPAIR):
        osb_ref[:, _lanes(p)] = acc_ref[p].astype(osb_ref.dtype)


def _mixer(x, kt, v, g_mix, q_norm_g, conv_w, w_in, w_conv_out, w_x_out):
    b, s, d = x.shape
    m = v.shape[1]
    tm = MIX_TM
    nblk = s // SB_T
    w2 = HEADS_PER_PAIR * SB_T
    tok = lambda width: pl.BlockSpec((None, tm, width), lambda i, j: (i, j, 0))
    return pl.pallas_call(
        _mixer_kernel,
        grid=(b, s // tm),
        in_specs=[
            tok(d),
            pl.BlockSpec((None, X_DIM, m), lambda i, j: (i, 0, 0), pipeline_mode=pl.Buffered(1)),
            pl.BlockSpec((None, m, X_DIM), lambda i, j: (i, 0, 0), pipeline_mode=pl.Buffered(1)),
            _resident((1, d)),
            _resident((1, X_HEAD_DIM)),
            _resident((CONV_WIDTH, CONV_DIM)),
            _resident((d, IN_COLS)),
            _resident((CONV_DIM, d)),
            _resident((X_DIM, d)),
        ],
        out_specs=[tok(SB_DIM), tok(d), tok(d)],
        out_shape=[
            jax.ShapeDtypeStruct((b, s, SB_DIM), BF16),
            jax.ShapeDtypeStruct((b, s, d), BF16),
            jax.ShapeDtypeStruct((b, s, d), BF16),
        ],
        scratch_shapes=[
            pltpu.VMEM((tm + SUBLANES, CONV_DIM), F32),
            pltpu.VMEM((tm, SB_DIM), BF16),
            pltpu.VMEM((SB_NPAIR, nblk + SB_PAD, LANES, SB_T), BF16),
            pltpu.VMEM((SB_NPAIR, nblk + SB_PAD, SB_T, LANES), BF16),
            pltpu.VMEM((w2, w2), BF16),
            pltpu.VMEM((SB_NPAIR, tm, w2), F32),
            pltpu.VMEM((SB_NPAIR, tm, LANES), F32),
        ],
        compiler_params=pltpu.CompilerParams(
            dimension_semantics=("arbitrary", "arbitrary"), vmem_limit_bytes=VMEM_LIMIT),
        name="mixer",
    )(x, kt, v, g_mix, q_norm_g, conv_w, w_in, w_conv_out, w_x_out)


def _post_kernel(x_ref, osb_ref, part_ref, gsb_ref, gmlp_ref, wsb_ref, wout_ref, wup_ref, wdn_ref,
                 out_ref):
    y_sb = _dot(osb_ref[...], wsb_ref[...])
    merged = part_ref[...] + gsb_ref[...] * y_sb
    x1 = x_ref[...] + _dot(merged.astype(BF16), wout_ref[...])
    h2 = _rms(x1, gmlp_ref[...]).astype(BF16)
    acc = x1
    for cidx in range(D_FF // FF_CHUNK):
        lo = cidx * FF_CHUNK
        up = jnp.maximum(_dot(h2, wup_ref[:, lo:lo + FF_CHUNK]), 0.0)
        acc = acc + _dot((up * up).astype(BF16), wdn_ref[lo:lo + FF_CHUNK, :])
    out_ref[...] = acc


def _post(x, o_sb, part, g_sb, g_mlp, w_sb_out, w_out, w_up, w_down):
    b, s, d = x.shape
    tm = POST_TM
    tok = lambda width: pl.BlockSpec((None, tm, width), lambda i, j: (i, j, 0))
    return pl.pallas_call(
        _post_kernel,
        grid=(b, s // tm),
        in_specs=[
            tok(d), tok(SB_DIM), tok(d), tok(d),
            _resident((1, d)),
            _resident((SB_DIM, d)),
            _resident((d, d)),
            _resident((d, D_FF)),
            _resident((D_FF, d)),
        ],
        out_specs=tok(d),
        out_shape=jax.ShapeDtypeStruct((b, s, d), F32),
        compiler_params=pltpu.CompilerParams(
            dimension_semantics=("arbitrary", "arbitrary"), vmem_limit_bytes=VMEM_LIMIT),
        name="post",
    )(x, o_sb, part, g_sb, g_mlp, w_sb_out, w_out, w_up, w_down)


def kernel(x, mem, g_mix, g_mem, w_in, conv_w, w_conv_out, w_sb_out, q_norm_g, k_norm_g,
           w_mem_kv, w_x_out, w_out, g_mlp, w_up, w_down):
    depth = w_in.shape[0]
    for l in range(depth):
        bf = lambda w: w[l].astype(BF16)
        row = lambda g: g[l][None, :]
        kt, v = _mem_kv(mem, row(g_mem), row(k_norm_g), bf(w_mem_kv))
        o_sb, part, g_sb = _mixer(
            x, kt, v, row(g_mix), row(q_norm_g), conv_w[l], bf(w_in), bf(w_conv_out), bf(w_x_out))
        x = _post(x, o_sb, part, g_sb, row(g_mlp), bf(w_sb_out), bf(w_out), bf(w_up), bf(w_down))
    return x
```

```python
import functools

import jax
import jax.numpy as jnp
from jax import lax
from jax.experimental import pallas as pl
from jax.experimental.pallas import tpu as pltpu

D_MODEL = 1024
CONV_DIM = D_MODEL
CONV_WIDTH = 3
SB_HEADS = 16
SB_HEAD_DIM = 64
SB_DIM = SB_HEADS * SB_HEAD_DIM
X_HEADS = 4
X_HEAD_DIM = 256
X_DIM = X_HEADS * X_HEAD_DIM
N_BRANCH = 3
D_FF = 4 * D_MODEL
EPS = 1e-6

OFF_CH = 0
OFF_CB = OFF_CH + CONV_DIM
OFF_CC = OFF_CB + CONV_DIM
OFF_SQ = OFF_CC + CONV_DIM
OFF_SK = OFF_SQ + SB_DIM
OFF_SV = OFF_SK + SB_DIM
OFF_XQ = OFF_SV + SB_DIM
OFF_GATE = OFF_XQ + X_DIM
IN_COLS = OFF_GATE + N_BRANCH * D_MODEL

LANES = 128
SUBLANES = 8
VMEM_LIMIT = 60 * 1024 * 1024

SB_T = 128
SB_R = 2
MIX_TM = SB_R * SB_T
POST_TM = 512
FF_CHUNK = 1024
SB_NPAIR = SB_DIM // LANES
SB_GROUP = 2
SB_PAD = 2
HEADS_PER_PAIR = LANES // SB_HEAD_DIM
LOG2E = 1.4426950408889634
SB_K_SCALE = SB_HEAD_DIM ** -0.5 * LOG2E
SB_DEAD = -152.0
SB_Z_MAX = 126.0

BF16 = jnp.bfloat16
F32 = jnp.float32


def _dot(a, b):
    return jnp.dot(a, b, preferred_element_type=F32)


def _rms(x, g):
    return x * lax.rsqrt(jnp.mean(x * x, axis=-1, keepdims=True) + EPS) * g


def _resident(shape):
    return pl.BlockSpec(shape, lambda *_: (0,) * len(shape), pipeline_mode=pl.Buffered(1))


def _lanes(p):
    return slice(p * LANES, (p + 1) * LANES)


def _mem_kv_kernel(mem_ref, gmem_ref, kg_ref, wkv_ref, kt_ref, v_ref):
    mn = _rms(mem_ref[...], gmem_ref[...]).astype(BF16)
    kv = _dot(mn, wkv_ref[...])
    for hh in range(X_HEADS):
        lo = hh * X_HEAD_DIM
        kn = _rms(kv[:, lo:lo + X_HEAD_DIM], kg_ref[...])
        kt_ref[lo:lo + X_HEAD_DIM, :] = kn.T.astype(BF16)
    v_ref[...] = kv[:, X_DIM:].astype(BF16)


def _mem_kv(mem, g_mem, k_norm_g, w_mem_kv):
    b, m, d = mem.shape
    return pl.pallas_call(
        _mem_kv_kernel,
        grid=(b,),
        in_specs=[
            pl.BlockSpec((None, m, d), lambda i: (i, 0, 0)),
            _resident((1, d)),
            _resident((1, X_HEAD_DIM)),
            _resident((d, 2 * X_DIM)),
        ],
        out_specs=[
            pl.BlockSpec((None, X_DIM, m), lambda i: (i, 0, 0)),
            pl.BlockSpec((None, m, X_DIM), lambda i: (i, 0, 0)),
        ],
        out_shape=[
            jax.ShapeDtypeStruct((b, X_DIM, m), BF16),
            jax.ShapeDtypeStruct((b, m, X_DIM), BF16),
        ],
        compiler_params=pltpu.CompilerParams(
            dimension_semantics=("arbitrary",), vmem_limit_bytes=VMEM_LIMIT),
        name="mem_kv",
    )(mem, g_mem, k_norm_g, w_mem_kv)


def _mixer_kernel(x_ref, kt_ref, v_ref, gmix_ref, qg_ref, convw_ref, win_ref, wco_ref, wxo_ref,
                  osb_ref, part_ref, gsb_ref,
                  ubuf, q_scr, kth, vh, tri_ref, carry_ref, acc_ref):
    t = SB_T
    tq = MIX_TM
    w2 = HEADS_PER_PAIR * t
    tile = pl.program_id(1)
    ja = 2 * tile

    @pl.when(tile == 0)
    def _():
        ubuf[0:SUBLANES, :] = jnp.zeros((SUBLANES, CONV_DIM), F32)
        for p in range(SB_NPAIR):
            kth[p, 0:SB_PAD] = jnp.zeros((SB_PAD, LANES, t), BF16)
            vh[p, 0:SB_PAD] = jnp.zeros((SB_PAD, t, LANES), BF16)

    h = _rms(x_ref[...], gmix_ref[...]).astype(BF16)

    def proj(off, width):
        return _dot(h, win_ref[:, off:off + width])

    q_scr[...] = proj(OFF_SQ, SB_DIM).astype(BF16)
    k_new = proj(OFF_SK, SB_DIM) * SB_K_SCALE
    v_new = proj(OFF_SV, SB_DIM).astype(BF16)
    for p in range(SB_NPAIR):
        for blk in range(SB_R):
            rows = slice(blk * t, (blk + 1) * t)
            kth[p, ja + SB_PAD + blk] = k_new[rows, _lanes(p)].T.astype(BF16)
            vh[p, ja + SB_PAD + blk] = v_new[rows, _lanes(p)]

    r = lax.broadcasted_iota(jnp.int32, (w2, w2), 0) % t
    c = lax.broadcasted_iota(jnp.int32, (w2, w2), 1)
    tri_ref[...] = jnp.where((c >= t) | (r >= c), -1.0, 0.0).astype(BF16)

    row = lax.broadcasted_iota(jnp.int32, (t, w2), 0)
    col = lax.broadcasted_iota(jnp.int32, (t, w2), 1) % t
    past = col < row

    feat_row = lax.broadcasted_iota(jnp.int32, (LANES, t), 0)
    feat_lane = lax.broadcasted_iota(jnp.int32, (t, LANES), 1)
    head_rows = [jnp.where(feat_row // SB_HEAD_DIM == hh, 1.0, 0.0).astype(BF16)
                 for hh in range(HEADS_PER_PAIR)]
    head_lanes = [jnp.where(feat_lane // SB_HEAD_DIM == hh, 1.0, 0.0).astype(BF16)
                  for hh in range(HEADS_PER_PAIR)]

    def mask_diag(x):
        head = jnp.where(past, x[:t], 0.0)
        return head if x.shape[0] == t else jnp.concatenate([head, x[t:]], axis=0)

    def put(full, r0, r1, new):
        parts = ([full[:r0]] if r0 > 0 else []) + [new] + ([full[r1:]] if r1 < full.shape[0] else [])
        return new if len(parts) == 1 else jnp.concatenate(parts, axis=0)

    def tiles_stages(tiles, carry, acc):
        carry, acc = dict(carry), dict(acc)
        zs, lhs = [], []
        for p, r0, r1, j, diag in tiles:
            kt = kth[p, j + SB_PAD]
            kt2 = jnp.concatenate([kt * m for m in head_rows], axis=1)
            zs.append(_dot(q_scr[r0:r1, _lanes(p)], kt2))
        yield None
        for z, (p, r0, r1, j, diag) in zip(zs, tiles):
            sp = jnp.maximum(z, jnp.log(1.0 + jnp.exp2(jnp.minimum(z, SB_Z_MAX))) * LOG2E)
            if diag:
                sp = mask_diag(sp)
            hi = sp.astype(BF16)
            lo = (sp - hi.astype(F32)).astype(BF16)
            lhs.append([jnp.concatenate([hi[:, hh * t:(hh + 1) * t], lo[:, hh * t:(hh + 1) * t]], axis=1)
                        for hh in range(HEADS_PER_PAIR)])
        res = [[_dot(x, tri_ref[...]) for x in xs] for xs in lhs]
        yield None
        avs = []
        for z, rs, (p, r0, r1, j, diag) in zip(zs, res, tiles):
            below = jnp.concatenate([x[:, :t] for x in rs], axis=1)
            total = jnp.concatenate([x[:, t:] for x in rs], axis=1)
            a = jnp.exp2(z + below + carry[p][r0:r1])
            if diag:
                a = mask_diag(a)
            avs.append(a.astype(BF16))
            carry[p] = put(carry[p], r0, r1, carry[p][r0:r1] + total)
        for a, (p, r0, r1, j, diag) in zip(avs, tiles):
            vb = vh[p, j + SB_PAD]
            v2 = jnp.concatenate([vb * m for m in head_lanes], axis=0)
            acc[p] = put(acc[p], r0, r1, acc[p][r0:r1] + _dot(a, v2))
        yield carry, acc

    half = t // 2
    steps = [(t, tq, ja + 1, True), (0, tq, ja, True), (0, t + half, ja - 1, False), (0, half, ja - 2, False)]
    groups = [range(g, g + SB_GROUP) for g in range(0, SB_NPAIR, SB_GROUP)]
    sweeps = [tiles_stages([(p,) + st for st in steps for p in grp],
                           {p: jnp.zeros((tq, w2), F32) for p in grp},
                           {p: jnp.zeros((tq, LANES), F32) for p in grp}) for grp in groups]

    def token_local():
        u = proj(OFF_CC, CONV_DIM) * proj(OFF_CH, CONV_DIM)
        ubuf[SUBLANES:SUBLANES + tq, :] = u
        cw = convw_ref[...]
        conv = (cw[0:1, :] * ubuf[SUBLANES - 2:SUBLANES - 2 + tq, :]
                + cw[1:2, :] * ubuf[SUBLANES - 1:SUBLANES - 1 + tq, :]
                + cw[2:3, :] * u)
        ubuf[0:SUBLANES, :] = ubuf[tq:tq + SUBLANES, :]
        yield
        y_conv = _dot((proj(OFF_CB, CONV_DIM) * conv).astype(BF16), wco_ref[...])
        yield
        xq = proj(OFF_XQ, X_DIM)
        heads = []
        for hh in range(X_HEADS):
            lo = hh * X_HEAD_DIM
            qn = _rms(xq[:, lo:lo + X_HEAD_DIM], qg_ref[...]).astype(BF16)
            s = _dot(qn, kt_ref[lo:lo + X_HEAD_DIM, :]) * (X_HEAD_DIM ** -0.5)
            pr = jnp.exp(s - jnp.max(s, axis=-1, keepdims=True))
            l = jnp.sum(pr, axis=-1, keepdims=True)
            heads.append(_dot(pr.astype(BF16), v_ref[:, lo:lo + X_HEAD_DIM]) / l)
            if hh % 2 == 1:
                yield
        y_x = _dot(jnp.concatenate(heads, axis=-1).astype(BF16), wxo_ref[...])
        yield
        g_conv = jax.nn.sigmoid(proj(OFF_GATE, D_MODEL))
        yield
        g_x = jax.nn.sigmoid(proj(OFF_GATE + 2 * D_MODEL, D_MODEL))
        part_ref[...] = (g_conv * y_conv + g_x * y_x).astype(part_ref.dtype)
        yield
        gsb_ref[...] = jax.nn.sigmoid(proj(OFF_GATE + D_MODEL, D_MODEL)).astype(gsb_ref.dtype)

    local = token_local()
    pieces_per_stage = SB_GROUP // 2
    state = []
    for sweep in sweeps:
        next(sweep)
        for _ in range(pieces_per_stage):
            next(local, None)
        next(sweep)
        for _ in range(pieces_per_stage):
            next(local, None)
        state.append(next(sweep))
    for _ in local:
        pass

    carry = {p: c for cs, _ in state for p, c in cs.items()}
    acc = {p: a for _, accs in state for p, a in accs.items()}
    for p in range(SB_NPAIR):
        carry_ref[p] = carry[p]
        acc_ref[p] = acc[p]

    def alive(rows):
        return functools.reduce(jnp.maximum, [jnp.max(x) for x in rows]) >= SB_DEAD

    rest = ((0, half, ja - 3), (half, t + half, ja - 2), (t + half, tq, ja - 1))
    alive0 = [alive([carry[p][r0:r1] for p in range(SB_NPAIR)]) for r0, r1, _ in rest]
    for (r0, r1, j_next), go in zip(rest, alive0):
        def k_cond(st):
            return jnp.logical_and(st[0] >= 0, st[1])

        def k_body(st, r0=r0, r1=r1):
            *_, (cnew, anew) = tiles_stages([(p, r0, r1, st[0], False) for p in range(SB_NPAIR)],
                                            {p: carry_ref[p] for p in range(SB_NPAIR)},
                                            {p: acc_ref[p] for p in range(SB_NPAIR)})
            for p in range(SB_NPAIR):
                carry_ref[p] = cnew[p]
                acc_ref[p] = anew[p]
            return st[0] - 1, alive([cnew[p][r0:r1] for p in range(SB_NPAIR)])

        lax.while_loop(k_cond, k_body, (j_next, go))
    for p in range(SB_NPAIR):
        osb_ref[:, _lanes(p)] = acc_ref[p].astype(osb_ref.dtype)


def _mixer(x, kt, v, g_mix, q_norm_g, conv_w, w_in, w_conv_out, w_x_out):
    b, s, d = x.shape
    m = v.shape[1]
    tm = MIX_TM
    nblk = s // SB_T
    w2 = HEADS_PER_PAIR * SB_T
    tok = lambda width: pl.BlockSpec((None, tm, width), lambda i, j: (i, j, 0))
    return pl.pallas_call(
        _mixer_kernel,
        grid=(b, s // tm),
        in_specs=[
            tok(d),
            pl.BlockSpec((None, X_DIM, m), lambda i, j: (i, 0, 0), pipeline_mode=pl.Buffered(1)),
            pl.BlockSpec((None, m, X_DIM), lambda i, j: (i, 0, 0), pipeline_mode=pl.Buffered(1)),
            _resident((1, d)),
            _resident((1, X_HEAD_DIM)),
            _resident((CONV_WIDTH, CONV_DIM)),
            _resident((d, IN_COLS)),
            _resident((CONV_DIM, d)),
            _resident((X_DIM, d)),
        ],
        out_specs=[tok(SB_DIM), tok(d), tok(d)],
        out_shape=[
            jax.ShapeDtypeStruct((b, s, SB_DIM), BF16),
            jax.ShapeDtypeStruct((b, s, d), BF16),
            jax.ShapeDtypeStruct((b, s, d), BF16),
        ],
        scratch_shapes=[
            pltpu.VMEM((tm + SUBLANES, CONV_DIM), F32),
            pltpu.VMEM((tm, SB_DIM), BF16),
            pltpu.VMEM((SB_NPAIR, nblk + SB_PAD, LANES, SB_T), BF16),
            pltpu.VMEM((SB_NPAIR, nblk + SB_PAD, SB_T, LANES), BF16),
            pltpu.VMEM((w2, w2), BF16),
            pltpu.VMEM((SB_NPAIR, tm, w2), F32),
            pltpu.VMEM((SB_NPAIR, tm, LANES), F32),
        ],
        compiler_params=pltpu.CompilerParams(
            dimension_semantics=("arbitrary", "arbitrary"), vmem_limit_bytes=VMEM_LIMIT),
        name="mixer",
    )(x, kt, v, g_mix, q_norm_g, conv_w, w_in, w_conv_out, w_x_out)


def _post_kernel(x_ref, osb_ref, part_ref, gsb_ref, gmlp_ref, wsb_ref, wout_ref, wup_ref, wdn_ref,
                 out_ref):
    y_sb = _dot(osb_ref[...], wsb_ref[...])
    merged = part_ref[...] + gsb_ref[...] * y_sb
    x1 = x_ref[...] + _dot(merged.astype(BF16), wout_ref[...])
    h2 = _rms(x1, gmlp_ref[...]).astype(BF16)
    acc = x1
    for cidx in range(D_FF // FF_CHUNK):
        lo = cidx * FF_CHUNK
        up = jnp.maximum(_dot(h2, wup_ref[:, lo:lo + FF_CHUNK]), 0.0)
        acc = acc + _dot((up * up).astype(BF16), wdn_ref[lo:lo + FF_CHUNK, :])
    out_ref[...] = acc


def _post(x, o_sb, part, g_sb, g_mlp, w_sb_out, w_out, w_up, w_down):
    b, s, d = x.shape
    tm = POST_TM
    tok = lambda width: pl.BlockSpec((None, tm, width), lambda i, j: (i, j, 0))
    return pl.pallas_call(
        _post_kernel,
        grid=(b, s // tm),
        in_specs=[
            tok(d), tok(SB_DIM), tok(d), tok(d),
            _resident((1, d)),
            _resident((SB_DIM, d)),
            _resident((d, d)),
            _resident((d, D_FF)),
            _resident((D_FF, d)),
        ],
        out_specs=tok(d),
        out_shape=jax.ShapeDtypeStruct((b, s, d), F32),
        compiler_params=pltpu.CompilerParams(
            dimension_semantics=("arbitrary", "arbitrary"), vmem_limit_bytes=VMEM_LIMIT),
        name="post",
    )(x, o_sb, part, g_sb, g_mlp, w_sb_out, w_out, w_up, w_down)


def kernel(x, mem, g_mix, g_mem, w_in, conv_w, w_conv_out, w_sb_out, q_norm_g, k_norm_g,
           w_mem_kv, w_x_out, w_out, g_mlp, w_up, w_down):
    depth = w_in.shape[0]
    for l in range(depth):
        bf = lambda w: w[l].astype(BF16)
        row = lambda g: g[l][None, :]
        kt, v = _mem_kv(mem, row(g_mem), row(k_norm_g), bf(w_mem_kv))
        o_sb, part, g_sb = _mixer(
            x, kt, v, row(g_mix), row(q_norm_g), conv_w[l], bf(w_in), bf(w_conv_out), bf(w_x_out))
        x = _post(x, o_sb, part, g_sb, row(g_mlp), bf(w_sb_out), bf(w_out), bf(w_up), bf(w_down))
    return x
```

```python
import functools

import jax
import jax.numpy as jnp
from jax import lax
from jax.experimental import pallas as pl
from jax.experimental.pallas import tpu as pltpu

D_MODEL = 1024
CONV_DIM = D_MODEL
CONV_WIDTH = 3
SB_HEADS = 16
SB_HEAD_DIM = 64
SB_DIM = SB_HEADS * SB_HEAD_DIM
X_HEADS = 4
X_HEAD_DIM = 256
X_DIM = X_HEADS * X_HEAD_DIM
N_BRANCH = 3
D_FF = 4 * D_MODEL
EPS = 1e-6

OFF_CH = 0
OFF_CB = OFF_CH + CONV_DIM
OFF_CC = OFF_CB + CONV_DIM
OFF_SQ = OFF_CC + CONV_DIM
OFF_SK = OFF_SQ + SB_DIM
OFF_SV = OFF_SK + SB_DIM
OFF_XQ = OFF_SV + SB_DIM
OFF_GATE = OFF_XQ + X_DIM
IN_COLS = OFF_GATE + N_BRANCH * D_MODEL

LANES = 128
SUBLANES = 8
VMEM_LIMIT = 60 * 1024 * 1024

SB_T = 128
SB_R = 2
MIX_TM = SB_R * SB_T
POST_TM = 512
FF_CHUNK = 1024
SB_NPAIR = SB_DIM // LANES
SB_GROUP = 2
SB_PAD = 2
HEADS_PER_PAIR = LANES // SB_HEAD_DIM
LOG2E = 1.4426950408889634
SB_K_SCALE = SB_HEAD_DIM ** -0.5 * LOG2E
SB_DEAD = -152.0
SB_Z_MAX = 126.0

BF16 = jnp.bfloat16
F32 = jnp.float32


def _dot(a, b):
    return jnp.dot(a, b, preferred_element_type=F32)


def _rms(x, g):
    return x * lax.rsqrt(jnp.mean(x * x, axis=-1, keepdims=True) + EPS) * g


def _resident(shape):
    return pl.BlockSpec(shape, lambda *_: (0,) * len(shape), pipeline_mode=pl.Buffered(1))


def _lanes(p):
    return slice(p * LANES, (p + 1) * LANES)


def _cast_rider(weights, steps, step_of):
    ins, in_specs, out_specs, out_shapes = [], [], [], []
    for w in weights:
        rows, cols = w.shape
        assert rows % steps == 0, (rows, steps)
        spec = pl.BlockSpec((None, rows // steps, cols), lambda *idx: (step_of(*idx), 0, 0))
        ins.append(w.reshape(steps, rows // steps, cols))
        in_specs.append(spec)
        out_specs.append(spec)
        out_shapes.append(jax.ShapeDtypeStruct((steps, rows // steps, cols), BF16))
    return ins, in_specs, out_specs, out_shapes


def _cast_refs(srcs, dsts):
    for src, dst in zip(srcs, dsts):
        dst[...] = src[...].astype(dst.dtype)


N_MEM_RIDERS = 3


def _mem_kv_kernel(mem_ref, gmem_ref, kg_ref, wkv_ref, *refs):
    cast_in, (kt_ref, v_ref), cast_out = refs[:N_MEM_RIDERS], refs[N_MEM_RIDERS:N_MEM_RIDERS + 2], refs[N_MEM_RIDERS + 2:]
    _cast_refs(cast_in, cast_out)
    mn = _rms(mem_ref[...], gmem_ref[...]).astype(BF16)
    kv = _dot(mn, wkv_ref[...].astype(BF16))
    for hh in range(X_HEADS):
        lo = hh * X_HEAD_DIM
        kn = _rms(kv[:, lo:lo + X_HEAD_DIM], kg_ref[...])
        kt_ref[lo:lo + X_HEAD_DIM, :] = kn.T.astype(BF16)
    v_ref[...] = kv[:, X_DIM:].astype(BF16)


def _mem_kv(mem, g_mem, k_norm_g, w_mem_kv, riders):
    b, m, d = mem.shape
    assert len(riders) == N_MEM_RIDERS
    r_in, r_in_specs, r_out_specs, r_out_shapes = _cast_rider(riders, b, lambda i: i)
    outs = pl.pallas_call(
        _mem_kv_kernel,
        grid=(b,),
        in_specs=[
            pl.BlockSpec((None, m, d), lambda i: (i, 0, 0)),
            _resident((1, d)),
            _resident((1, X_HEAD_DIM)),
            _resident((d, 2 * X_DIM)),
        ] + r_in_specs,
        out_specs=[
            pl.BlockSpec((None, X_DIM, m), lambda i: (i, 0, 0)),
            pl.BlockSpec((None, m, X_DIM), lambda i: (i, 0, 0)),
        ] + r_out_specs,
        out_shape=[
            jax.ShapeDtypeStruct((b, X_DIM, m), BF16),
            jax.ShapeDtypeStruct((b, m, X_DIM), BF16),
        ] + r_out_shapes,
        compiler_params=pltpu.CompilerParams(
            dimension_semantics=("arbitrary",), vmem_limit_bytes=VMEM_LIMIT),
        name="mem_kv",
    )(mem, g_mem, k_norm_g, w_mem_kv, *r_in)
    return outs[0], outs[1], [o.reshape(w.shape) for o, w in zip(outs[2:], riders)]


N_MIX_RIDERS = 4


def _mixer_kernel(x_ref, kt_ref, v_ref, gmix_ref, qg_ref, convw_ref, win_ref, wco_ref, wxo_ref, *refs):
    cast_in, refs = refs[:N_MIX_RIDERS], refs[N_MIX_RIDERS:]
    (osb_ref, part_ref, gsb_ref), refs = refs[:3], refs[3:]
    cast_out, (ubuf, q_scr, kth, vh, tri_ref, carry_ref, acc_ref) = refs[:N_MIX_RIDERS], refs[N_MIX_RIDERS:]
    _cast_refs(cast_in, cast_out)
    t = SB_T
    tq = MIX_TM
    w2 = HEADS_PER_PAIR * t
    tile = pl.program_id(1)
    ja = 2 * tile

    @pl.when(tile == 0)
    def _():
        ubuf[0:SUBLANES, :] = jnp.zeros((SUBLANES, CONV_DIM), F32)
        for p in range(SB_NPAIR):
            kth[p, 0:SB_PAD] = jnp.zeros((SB_PAD, LANES, t), BF16)
            vh[p, 0:SB_PAD] = jnp.zeros((SB_PAD, t, LANES), BF16)

    h = _rms(x_ref[...], gmix_ref[...]).astype(BF16)

    def proj(off, width):
        return _dot(h, win_ref[:, off:off + width])

    q_scr[...] = proj(OFF_SQ, SB_DIM).astype(BF16)
    k_new = proj(OFF_SK, SB_DIM) * SB_K_SCALE
    v_new = proj(OFF_SV, SB_DIM).astype(BF16)
    for p in range(SB_NPAIR):
        for blk in range(SB_R):
            rows = slice(blk * t, (blk + 1) * t)
            kth[p, ja + SB_PAD + blk] = k_new[rows, _lanes(p)].T.astype(BF16)
            vh[p, ja + SB_PAD + blk] = v_new[rows, _lanes(p)]

    r = lax.broadcasted_iota(jnp.int32, (w2, w2), 0) % t
    c = lax.broadcasted_iota(jnp.int32, (w2, w2), 1)
    tri_ref[...] = jnp.where((c >= t) | (r >= c), -1.0, 0.0).astype(BF16)

    row = lax.broadcasted_iota(jnp.int32, (t, w2), 0)
    col = lax.broadcasted_iota(jnp.int32, (t, w2), 1) % t
    past = col < row

    feat_row = lax.broadcasted_iota(jnp.int32, (LANES, t), 0)
    feat_lane = lax.broadcasted_iota(jnp.int32, (t, LANES), 1)
    head_rows = [jnp.where(feat_row // SB_HEAD_DIM == hh, 1.0, 0.0).astype(BF16)
                 for hh in range(HEADS_PER_PAIR)]
    head_lanes = [jnp.where(feat_lane // SB_HEAD_DIM == hh, 1.0, 0.0).astype(BF16)
                  for hh in range(HEADS_PER_PAIR)]

    def mask_diag(x):
        head = jnp.where(past, x[:t], 0.0)
        return head if x.shape[0] == t else jnp.concatenate([head, x[t:]], axis=0)

    def put(full, r0, r1, new):
        parts = ([full[:r0]] if r0 > 0 else []) + [new] + ([full[r1:]] if r1 < full.shape[0] else [])
        return new if len(parts) == 1 else jnp.concatenate(parts, axis=0)

    def tiles_stages(tiles, carry, acc):
        carry, acc = dict(carry), dict(acc)
        zs, lhs = [], []
        for p, r0, r1, j, diag in tiles:
            kt = kth[p, j + SB_PAD]
            kt2 = jnp.concatenate([kt * m for m in head_rows], axis=1)
            zs.append(_dot(q_scr[r0:r1, _lanes(p)], kt2))
        yield None
        for z, (p, r0, r1, j, diag) in zip(zs, tiles):
            sp = jnp.maximum(z, jnp.log(1.0 + jnp.exp2(jnp.minimum(z, SB_Z_MAX))) * LOG2E)
            if diag:
                sp = mask_diag(sp)
            hi = sp.astype(BF16)
            lo = (sp - hi.astype(F32)).astype(BF16)
            lhs.append([jnp.concatenate([hi[:, hh * t:(hh + 1) * t], lo[:, hh * t:(hh + 1) * t]], axis=1)
                        for hh in range(HEADS_PER_PAIR)])
        res = [[_dot(x, tri_ref[...]) for x in xs] for xs in lhs]
        yield None
        avs = []
        for z, rs, (p, r0, r1, j, diag) in zip(zs, res, tiles):
            below = jnp.concatenate([x[:, :t] for x in rs], axis=1)
            total = jnp.concatenate([x[:, t:] for x in rs], axis=1)
            a = jnp.exp2(z + below + carry[p][r0:r1])
            if diag:
                a = mask_diag(a)
            avs.append(a.astype(BF16))
            carry[p] = put(carry[p], r0, r1, carry[p][r0:r1] + total)
        for a, (p, r0, r1, j, diag) in zip(avs, tiles):
            vb = vh[p, j + SB_PAD]
            v2 = jnp.concatenate([vb * m for m in head_lanes], axis=0)
            acc[p] = put(acc[p], r0, r1, acc[p][r0:r1] + _dot(a, v2))
        yield carry, acc

    half = t // 2
    steps = [(t, tq, ja + 1, True), (0, tq, ja, True), (0, t + half, ja - 1, False), (0, half, ja - 2, False)]
    groups = [range(g, g + SB_GROUP) for g in range(0, SB_NPAIR, SB_GROUP)]
    sweeps = [tiles_stages([(p,) + st for st in steps for p in grp],
                           {p: jnp.zeros((tq, w2), F32) for p in grp},
                           {p: jnp.zeros((tq, LANES), F32) for p in grp}) for grp in groups]

    def token_local():
        u = proj(OFF_CC, CONV_DIM) * proj(OFF_CH, CONV_DIM)
        ubuf[SUBLANES:SUBLANES + tq, :] = u
        cw = convw_ref[...]
        conv = (cw[0:1, :] * ubuf[SUBLANES - 2:SUBLANES - 2 + tq, :]
                + cw[1:2, :] * ubuf[SUBLANES - 1:SUBLANES - 1 + tq, :]
                + cw[2:3, :] * u)
        ubuf[0:SUBLANES, :] = ubuf[tq:tq + SUBLANES, :]
        yield
        y_conv = _dot((proj(OFF_CB, CONV_DIM) * conv).astype(BF16), wco_ref[...])
        yield
        xq = proj(OFF_XQ, X_DIM)
        heads = []
        for hh in range(X_HEADS):
            lo = hh * X_HEAD_DIM
            qn = _rms(xq[:, lo:lo + X_HEAD_DIM], qg_ref[...]).astype(BF16)
            s = _dot(qn, kt_ref[lo:lo + X_HEAD_DIM, :]) * (X_HEAD_DIM ** -0.5)
            pr = jnp.exp(s - jnp.max(s, axis=-1, keepdims=True))
            l = jnp.sum(pr, axis=-1, keepdims=True)
            heads.append(_dot(pr.astype(BF16), v_ref[:, lo:lo + X_HEAD_DIM]) / l)
            if hh % 2 == 1:
                yield
        y_x = _dot(jnp.concatenate(heads, axis=-1).astype(BF16), wxo_ref[...])
        yield
        g_conv = jax.nn.sigmoid(proj(OFF_GATE, D_MODEL))
        yield
        g_x = jax.nn.sigmoid(proj(OFF_GATE + 2 * D_MODEL, D_MODEL))
        part_ref[...] = (g_conv * y_conv + g_x * y_x).astype(part_ref.dtype)
        yield
        gsb_ref[...] = jax.nn.sigmoid(proj(OFF_GATE + D_MODEL, D_MODEL)).astype(gsb_ref.dtype)

    local = token_local()
    pieces_per_stage = SB_GROUP // 2
    state = []
    for sweep in sweeps:
        next(sweep)
        for _ in range(pieces_per_stage):
            next(local, None)
        next(sweep)
        for _ in range(pieces_per_stage):
            next(local, None)
        state.append(next(sweep))
    for _ in local:
        pass

    carry = {p: c for cs, _ in state for p, c in cs.items()}
    acc = {p: a for _, accs in state for p, a in accs.items()}
    for p in range(SB_NPAIR):
        carry_ref[p] = carry[p]
        acc_ref[p] = acc[p]

    def alive(rows):
        return functools.reduce(jnp.maximum, [jnp.max(x) for x in rows]) >= SB_DEAD

    rest = ((0, half, ja - 3), (half, t + half, ja - 2), (t + half, tq, ja - 1))
    alive0 = [alive([carry[p][r0:r1] for p in range(SB_NPAIR)]) for r0, r1, _ in rest]
    for (r0, r1, j_next), go in zip(rest, alive0):
        def k_cond(st):
            return jnp.logical_and(st[0] >= 0, st[1])

        def k_body(st, r0=r0, r1=r1):
            *_, (cnew, anew) = tiles_stages([(p, r0, r1, st[0], False) for p in range(SB_NPAIR)],
                                            {p: carry_ref[p] for p in range(SB_NPAIR)},
                                            {p: acc_ref[p] for p in range(SB_NPAIR)})
            for p in range(SB_NPAIR):
                carry_ref[p] = cnew[p]
                acc_ref[p] = anew[p]
            return st[0] - 1, alive([cnew[p][r0:r1] for p in range(SB_NPAIR)])

        lax.while_loop(k_cond, k_body, (j_next, go))
    for p in range(SB_NPAIR):
        osb_ref[:, _lanes(p)] = acc_ref[p].astype(osb_ref.dtype)


def _mixer(x, kt, v, g_mix, q_norm_g, conv_w, w_in, w_conv_out, w_x_out, riders):
    b, s, d = x.shape
    m = v.shape[1]
    tm = MIX_TM
    nt = s // tm
    nblk = s // SB_T
    w2 = HEADS_PER_PAIR * SB_T
    assert len(riders) == N_MIX_RIDERS
    r_in, r_in_specs, r_out_specs, r_out_shapes = _cast_rider(riders, b * nt, lambda i, j: i * nt + j)
    tok = lambda width: pl.BlockSpec((None, tm, width), lambda i, j: (i, j, 0))
    outs = pl.pallas_call(
        _mixer_kernel,
        grid=(b, s // tm),
        in_specs=[
            tok(d),
            pl.BlockSpec((None, X_DIM, m), lambda i, j: (i, 0, 0), pipeline_mode=pl.Buffered(1)),
            pl.BlockSpec((None, m, X_DIM), lambda i, j: (i, 0, 0), pipeline_mode=pl.Buffered(1)),
            _resident((1, d)),
            _resident((1, X_HEAD_DIM)),
            _resident((CONV_WIDTH, CONV_DIM)),
            _resident((d, IN_COLS)),
            _resident((CONV_DIM, d)),
            _resident((X_DIM, d)),
        ] + r_in_specs,
        out_specs=[tok(SB_DIM), tok(d), tok(d)] + r_out_specs,
        out_shape=[
            jax.ShapeDtypeStruct((b, s, SB_DIM), BF16),
            jax.ShapeDtypeStruct((b, s, d), BF16),
            jax.ShapeDtypeStruct((b, s, d), BF16),
        ] + r_out_shapes,
        scratch_shapes=[
            pltpu.VMEM((tm + SUBLANES, CONV_DIM), F32),
            pltpu.VMEM((tm, SB_DIM), BF16),
            pltpu.VMEM((SB_NPAIR, nblk + SB_PAD, LANES, SB_T), BF16),
            pltpu.VMEM((SB_NPAIR, nblk + SB_PAD, SB_T, LANES), BF16),
            pltpu.VMEM((w2, w2), BF16),
            pltpu.VMEM((SB_NPAIR, tm, w2), F32),
            pltpu.VMEM((SB_NPAIR, tm, LANES), F32),
        ],
        compiler_params=pltpu.CompilerParams(
            dimension_semantics=("arbitrary", "arbitrary"), vmem_limit_bytes=VMEM_LIMIT),
        name="mixer",
    )(x, kt, v, g_mix, q_norm_g, conv_w, w_in, w_conv_out, w_x_out, *r_in)
    return outs[0], outs[1], outs[2], [o.reshape(w.shape) for o, w in zip(outs[3:], riders)]


def _post_kernel(x_ref, osb_ref, part_ref, gsb_ref, gmlp_ref, wsb_ref, wout_ref, wup_ref, wdn_ref,
                 out_ref):
    y_sb = _dot(osb_ref[...], wsb_ref[...])
    merged = part_ref[...] + gsb_ref[...] * y_sb
    x1 = x_ref[...] + _dot(merged.astype(BF16), wout_ref[...])
    h2 = _rms(x1, gmlp_ref[...]).astype(BF16)
    acc = x1
    for cidx in range(D_FF // FF_CHUNK):
        lo = cidx * FF_CHUNK
        up = jnp.maximum(_dot(h2, wup_ref[:, lo:lo + FF_CHUNK]), 0.0)
        acc = acc + _dot((up * up).astype(BF16), wdn_ref[lo:lo + FF_CHUNK, :])
    out_ref[...] = acc


def _post(x, o_sb, part, g_sb, g_mlp, w_sb_out, w_out, w_up, w_down):
    b, s, d = x.shape
    tm = POST_TM
    tok = lambda width: pl.BlockSpec((None, tm, width), lambda i, j: (i, j, 0))
    return pl.pallas_call(
        _post_kernel,
        grid=(b, s // tm),
        in_specs=[
            tok(d), tok(SB_DIM), tok(d), tok(d),
            _resident((1, d)),
            _resident((SB_DIM, d)),
            _resident((d, d)),
            _resident((d, D_FF)),
            _resident((D_FF, d)),
        ],
        out_specs=tok(d),
        out_shape=jax.ShapeDtypeStruct((b, s, d), F32),
        compiler_params=pltpu.CompilerParams(
            dimension_semantics=("arbitrary", "arbitrary"), vmem_limit_bytes=VMEM_LIMIT),
        name="post",
    )(x, o_sb, part, g_sb, g_mlp, w_sb_out, w_out, w_up, w_down)


def kernel(x, mem, g_mix, g_mem, w_in, conv_w, w_conv_out, w_sb_out, q_norm_g, k_norm_g,
           w_mem_kv, w_x_out, w_out, g_mlp, w_up, w_down):
    depth = w_in.shape[0]
    for l in range(depth):
        row = lambda g: g[l][None, :]
        kt, v, mixer_w = _mem_kv(mem, row(g_mem), row(k_norm_g), w_mem_kv[l],
                                 (w_in[l], w_conv_out[l], w_x_out[l]))
        o_sb, part, g_sb, post_w = _mixer(
            x, kt, v, row(g_mix), row(q_norm_g), conv_w[l], *mixer_w,
            (w_sb_out[l], w_out[l], w_up[l], w_down[l]))
        x = _post(x, o_sb, part, g_sb, row(g_mlp), *post_w)
    return x
```

```python
import functools

import jax
import jax.numpy as jnp
from jax import lax
from jax.experimental import pallas as pl
from jax.experimental.pallas import tpu as pltpu

D_MODEL = 1024
CONV_DIM = D_MODEL
CONV_WIDTH = 3
SB_HEADS = 16
SB_HEAD_DIM = 64
SB_DIM = SB_HEADS * SB_HEAD_DIM
X_HEADS = 4
X_HEAD_DIM = 256
X_DIM = X_HEADS * X_HEAD_DIM
N_BRANCH = 3
D_FF = 4 * D_MODEL
EPS = 1e-6

OFF_CH = 0
OFF_CB = OFF_CH + CONV_DIM
OFF_CC = OFF_CB + CONV_DIM
OFF_SQ = OFF_CC + CONV_DIM
OFF_SK = OFF_SQ + SB_DIM
OFF_SV = OFF_SK + SB_DIM
OFF_XQ = OFF_SV + SB_DIM
OFF_GATE = OFF_XQ + X_DIM
IN_COLS = OFF_GATE + N_BRANCH * D_MODEL

LANES = 128
SUBLANES = 8
VMEM_LIMIT = 60 * 1024 * 1024

SB_T = 128
SB_R = 2
MIX_TM = SB_R * SB_T
POST_TM = 512
FF_CHUNK = 1024
SB_NPAIR = SB_DIM // LANES
SB_GROUP = 2
SB_PAD = 2
SB_EARLY = 48
HEADS_PER_PAIR = LANES // SB_HEAD_DIM
LOG2E = 1.4426950408889634
SB_K_SCALE = SB_HEAD_DIM ** -0.5 * LOG2E
SB_DEAD = -152.0
SB_Z_MAX = 126.0

BF16 = jnp.bfloat16
F32 = jnp.float32


def _dot(a, b):
    return jnp.dot(a, b, preferred_element_type=F32)


def _rms(x, g):
    return x * lax.rsqrt(jnp.mean(x * x, axis=-1, keepdims=True) + EPS) * g


def _resident(shape):
    return pl.BlockSpec(shape, lambda *_: (0,) * len(shape), pipeline_mode=pl.Buffered(1))


def _lanes(p):
    return slice(p * LANES, (p + 1) * LANES)


def _cast_rider(weights, steps, step_of):
    ins, in_specs, out_specs, out_shapes = [], [], [], []
    for w in weights:
        rows, cols = w.shape
        assert rows % steps == 0, (rows, steps)
        spec = pl.BlockSpec((None, rows // steps, cols), lambda *idx: (step_of(*idx), 0, 0))
        ins.append(w.reshape(steps, rows // steps, cols))
        in_specs.append(spec)
        out_specs.append(spec)
        out_shapes.append(jax.ShapeDtypeStruct((steps, rows // steps, cols), BF16))
    return ins, in_specs, out_specs, out_shapes


def _cast_refs(srcs, dsts):
    for src, dst in zip(srcs, dsts):
        dst[...] = src[...].astype(dst.dtype)


N_MEM_RIDERS = 3


MEM_SPLIT = 4


def _mem_kv_kernel(mem_ref, gmem_ref, kg_ref, wkv_ref, *refs):
    cast_in, (kt_ref, v_ref), cast_out = refs[:N_MEM_RIDERS], refs[N_MEM_RIDERS:N_MEM_RIDERS + 2], refs[N_MEM_RIDERS + 2:]
    _cast_refs(cast_in, cast_out)

    @pl.when(pl.program_id(1) == 0)
    def _():
        mn = _rms(mem_ref[...], gmem_ref[...]).astype(BF16)
        kv = _dot(mn, wkv_ref[...].astype(BF16))
        for hh in range(X_HEADS):
            lo = hh * X_HEAD_DIM
            kn = _rms(kv[:, lo:lo + X_HEAD_DIM], kg_ref[...])
            kt_ref[lo:lo + X_HEAD_DIM, :] = kn.T.astype(BF16)
        v_ref[...] = kv[:, X_DIM:].astype(BF16)


def _mem_kv(mem, g_mem, k_norm_g, w_mem_kv, riders):
    b, m, d = mem.shape
    assert len(riders) == N_MEM_RIDERS
    r_in, r_in_specs, r_out_specs, r_out_shapes = _cast_rider(
        riders, b * MEM_SPLIT, lambda i, j: i * MEM_SPLIT + j)
    outs = pl.pallas_call(
        _mem_kv_kernel,
        grid=(b, MEM_SPLIT),
        in_specs=[
            pl.BlockSpec((None, m, d), lambda i, j: (i, 0, 0)),
            _resident((1, d)),
            _resident((1, X_HEAD_DIM)),
            _resident((d, 2 * X_DIM)),
        ] + r_in_specs,
        out_specs=[
            pl.BlockSpec((None, X_DIM, m), lambda i, j: (i, 0, 0)),
            pl.BlockSpec((None, m, X_DIM), lambda i, j: (i, 0, 0)),
        ] + r_out_specs,
        out_shape=[
            jax.ShapeDtypeStruct((b, X_DIM, m), BF16),
            jax.ShapeDtypeStruct((b, m, X_DIM), BF16),
        ] + r_out_shapes,
        compiler_params=pltpu.CompilerParams(
            dimension_semantics=("arbitrary", "arbitrary"), vmem_limit_bytes=VMEM_LIMIT),
        name="mem_kv",
    )(mem, g_mem, k_norm_g, w_mem_kv, *r_in)
    return outs[0], outs[1], [o.reshape(w.shape) for o, w in zip(outs[2:], riders)]


N_MIX_RIDERS = 4


def _mixer_kernel(x_ref, kt_ref, v_ref, gmix_ref, qg_ref, convw_ref, win_ref, wco_ref, wxo_ref, *refs):
    cast_in, refs = refs[:N_MIX_RIDERS], refs[N_MIX_RIDERS:]
    (osb_ref, part_ref, gsb_ref), refs = refs[:3], refs[3:]
    cast_out, (ubuf, q_scr, kth, vh, tri_ref, carry_ref, acc_ref) = refs[:N_MIX_RIDERS], refs[N_MIX_RIDERS:]
    _cast_refs(cast_in, cast_out)
    t = SB_T
    tq = MIX_TM
    w2 = HEADS_PER_PAIR * t
    tile = pl.program_id(1)
    ja = 2 * tile

    @pl.when(tile == 0)
    def _():
        ubuf[0:SUBLANES, :] = jnp.zeros((SUBLANES, CONV_DIM), F32)
        for p in range(SB_NPAIR):
            kth[p, 0:SB_PAD] = jnp.zeros((SB_PAD, LANES, t), BF16)
            vh[p, 0:SB_PAD] = jnp.zeros((SB_PAD, t, LANES), BF16)

    h = _rms(x_ref[...], gmix_ref[...]).astype(BF16)

    def proj(off, width):
        return _dot(h, win_ref[:, off:off + width])

    q_scr[...] = proj(OFF_SQ, SB_DIM).astype(BF16)
    k_new = proj(OFF_SK, SB_DIM) * SB_K_SCALE
    v_new = proj(OFF_SV, SB_DIM).astype(BF16)
    for p in range(SB_NPAIR):
        for blk in range(SB_R):
            rows = slice(blk * t, (blk + 1) * t)
            kth[p, ja + SB_PAD + blk] = k_new[rows, _lanes(p)].T.astype(BF16)
            vh[p, ja + SB_PAD + blk] = v_new[rows, _lanes(p)]

    r = lax.broadcasted_iota(jnp.int32, (w2, w2), 0) % t
    c = lax.broadcasted_iota(jnp.int32, (w2, w2), 1)
    tri_ref[...] = jnp.where((c >= t) | (r >= c), -1.0, 0.0).astype(BF16)

    row = lax.broadcasted_iota(jnp.int32, (t, w2), 0)
    col = lax.broadcasted_iota(jnp.int32, (t, w2), 1) % t
    past = col < row

    feat_row = lax.broadcasted_iota(jnp.int32, (LANES, t), 0)
    feat_lane = lax.broadcasted_iota(jnp.int32, (t, LANES), 1)
    head_rows = [jnp.where(feat_row // SB_HEAD_DIM == hh, 1.0, 0.0).astype(BF16)
                 for hh in range(HEADS_PER_PAIR)]
    head_lanes = [jnp.where(feat_lane // SB_HEAD_DIM == hh, 1.0, 0.0).astype(BF16)
                  for hh in range(HEADS_PER_PAIR)]

    def mask_diag(x):
        head = jnp.where(past, x[:t], 0.0)
        return head if x.shape[0] == t else jnp.concatenate([head, x[t:]], axis=0)

    def put(full, r0, r1, new):
        parts = ([full[:r0]] if r0 > 0 else []) + [new] + ([full[r1:]] if r1 < full.shape[0] else [])
        return new if len(parts) == 1 else jnp.concatenate(parts, axis=0)

    def tiles_stages(tiles, carry, acc):
        carry, acc = dict(carry), dict(acc)
        zs, lhs = [], []
        for p, r0, r1, j, diag in tiles:
            kt = kth[p, j + SB_PAD]
            kt2 = jnp.concatenate([kt * m for m in head_rows], axis=1)
            zs.append(_dot(q_scr[r0:r1, _lanes(p)], kt2))
        yield None
        for z, (p, r0, r1, j, diag) in zip(zs, tiles):
            sp = jnp.maximum(z, jnp.log(1.0 + jnp.exp2(jnp.minimum(z, SB_Z_MAX))) * LOG2E)
            if diag:
                sp = mask_diag(sp)
            hi = sp.astype(BF16)
            lo = (sp - hi.astype(F32)).astype(BF16)
            lhs.append([jnp.concatenate([hi[:, hh * t:(hh + 1) * t], lo[:, hh * t:(hh + 1) * t]], axis=1)
                        for hh in range(HEADS_PER_PAIR)])
        res = [[_dot(x, tri_ref[...]) for x in xs] for xs in lhs]
        yield None
        avs = []
        for z, rs, (p, r0, r1, j, diag) in zip(zs, res, tiles):
            below = jnp.concatenate([x[:, :t] for x in rs], axis=1)
            total = jnp.concatenate([x[:, t:] for x in rs], axis=1)
            a = jnp.exp2(z + below + carry[p][r0:r1])
            if diag:
                a = mask_diag(a)
            avs.append(a.astype(BF16))
            carry[p] = put(carry[p], r0, r1, carry[p][r0:r1] + total)
        for a, (p, r0, r1, j, diag) in zip(avs, tiles):
            vb = vh[p, j + SB_PAD]
            v2 = jnp.concatenate([vb * m for m in head_lanes], axis=0)
            acc[p] = put(acc[p], r0, r1, acc[p][r0:r1] + _dot(a, v2))
        yield carry, acc

    early = SB_EARLY
    steps = [(t, tq, ja + 1, True), (0, tq, ja, True), (0, t + early, ja - 1, False), (0, early, ja - 2, False)]
    groups = [range(g, g + SB_GROUP) for g in range(0, SB_NPAIR, SB_GROUP)]
    sweeps = [tiles_stages([(p,) + st for st in steps for p in grp],
                           {p: jnp.zeros((tq, w2), F32) for p in grp},
                           {p: jnp.zeros((tq, LANES), F32) for p in grp}) for grp in groups]

    def token_local():
        u = proj(OFF_CC, CONV_DIM) * proj(OFF_CH, CONV_DIM)
        ubuf[SUBLANES:SUBLANES + tq, :] = u
        cw = convw_ref[...]
        conv = (cw[0:1, :] * ubuf[SUBLANES - 2:SUBLANES - 2 + tq, :]
                + cw[1:2, :] * ubuf[SUBLANES - 1:SUBLANES - 1 + tq, :]
                + cw[2:3, :] * u)
        ubuf[0:SUBLANES, :] = ubuf[tq:tq + SUBLANES, :]
        yield
        y_conv = _dot((proj(OFF_CB, CONV_DIM) * conv).astype(BF16), wco_ref[...])
        yield
        xq = proj(OFF_XQ, X_DIM)
        heads = []
        for hh in range(X_HEADS):
            lo = hh * X_HEAD_DIM
            qn = _rms(xq[:, lo:lo + X_HEAD_DIM], qg_ref[...]).astype(BF16)
            s = _dot(qn, kt_ref[lo:lo + X_HEAD_DIM, :]) * (X_HEAD_DIM ** -0.5)
            pr = jnp.exp(s - jnp.max(s, axis=-1, keepdims=True))
            l = jnp.sum(pr, axis=-1, keepdims=True)
            heads.append(_dot(pr.astype(BF16), v_ref[:, lo:lo + X_HEAD_DIM]) / l)
            if hh % 2 == 1:
                yield
        y_x = _dot(jnp.concatenate(heads, axis=-1).astype(BF16), wxo_ref[...])
        yield
        g_conv = jax.nn.sigmoid(proj(OFF_GATE, D_MODEL))
        yield
        g_x = jax.nn.sigmoid(proj(OFF_GATE + 2 * D_MODEL, D_MODEL))
        part_ref[...] = (g_conv * y_conv + g_x * y_x).astype(part_ref.dtype)
        yield
        gsb_ref[...] = jax.nn.sigmoid(proj(OFF_GATE + D_MODEL, D_MODEL)).astype(gsb_ref.dtype)

    local = token_local()
    pieces_per_stage = SB_GROUP // 2
    state = []
    for sweep in sweeps:
        next(sweep)
        for _ in range(pieces_per_stage):
            next(local, None)
        next(sweep)
        for _ in range(pieces_per_stage):
            next(local, None)
        state.append(next(sweep))
    for _ in local:
        pass

    carry = {p: c for cs, _ in state for p, c in cs.items()}
    acc = {p: a for _, accs in state for p, a in accs.items()}
    for p in range(SB_NPAIR):
        carry_ref[p] = carry[p]
        acc_ref[p] = acc[p]
        osb_ref[:, _lanes(p)] = acc[p].astype(osb_ref.dtype)

    def alive(rows):
        return functools.reduce(jnp.maximum, [jnp.max(x) for x in rows]) >= SB_DEAD

    rest = ((0, early, ja - 3), (early, t + early, ja - 2), (t + early, tq, ja - 1))
    alive0 = [alive([carry[p][r0:r1] for p in range(SB_NPAIR)]) for r0, r1, _ in rest]
    for (r0, r1, j_next), go in zip(rest, alive0):
        def k_cond(st):
            return jnp.logical_and(st[0] >= 0, st[1])

        def k_body(st, r0=r0, r1=r1):
            *_, (cnew, anew) = tiles_stages([(p, r0, r1, st[0], False) for p in range(SB_NPAIR)],
                                            {p: carry_ref[p] for p in range(SB_NPAIR)},
                                            {p: acc_ref[p] for p in range(SB_NPAIR)})
            for p in range(SB_NPAIR):
                carry_ref[p] = cnew[p]
                acc_ref[p] = anew[p]
                osb_ref[r0:r1, _lanes(p)] = anew[p][r0:r1].astype(osb_ref.dtype)
            return st[0] - 1, alive([cnew[p][r0:r1] for p in range(SB_NPAIR)])

        lax.while_loop(k_cond, k_body, (j_next, go))


def _mixer(x, kt, v, g_mix, q_norm_g, conv_w, w_in, w_conv_out, w_x_out, riders):
    b, s, d = x.shape
    m = v.shape[1]
    tm = MIX_TM
    nt = s // tm
    nblk = s // SB_T
    w2 = HEADS_PER_PAIR * SB_T
    assert len(riders) == N_MIX_RIDERS
    r_in, r_in_specs, r_out_specs, r_out_shapes = _cast_rider(riders, b * nt, lambda i, j: i * nt + j)
    tok = lambda width: pl.BlockSpec((None, tm, width), lambda i, j: (i, j, 0))
    outs = pl.pallas_call(
        _mixer_kernel,
        grid=(b, s // tm),
        in_specs=[
            tok(d),
            pl.BlockSpec((None, X_DIM, m), lambda i, j: (i, 0, 0), pipeline_mode=pl.Buffered(1)),
            pl.BlockSpec((None, m, X_DIM), lambda i, j: (i, 0, 0), pipeline_mode=pl.Buffered(1)),
            _resident((1, d)),
            _resident((1, X_HEAD_DIM)),
            _resident((CONV_WIDTH, CONV_DIM)),
            _resident((d, IN_COLS)),
            _resident((CONV_DIM, d)),
            _resident((X_DIM, d)),
        ] + r_in_specs,
        out_specs=[tok(SB_DIM), tok(d), tok(d)] + r_out_specs,
        out_shape=[
            jax.ShapeDtypeStruct((b, s, SB_DIM), BF16),
            jax.ShapeDtypeStruct((b, s, d), BF16),
            jax.ShapeDtypeStruct((b, s, d), BF16),
        ] + r_out_shapes,
        scratch_shapes=[
            pltpu.VMEM((tm + SUBLANES, CONV_DIM), F32),
            pltpu.VMEM((tm, SB_DIM), BF16),
            pltpu.VMEM((SB_NPAIR, nblk + SB_PAD, LANES, SB_T), BF16),
            pltpu.VMEM((SB_NPAIR, nblk + SB_PAD, SB_T, LANES), BF16),
            pltpu.VMEM((w2, w2), BF16),
            pltpu.VMEM((SB_NPAIR, tm, w2), F32),
            pltpu.VMEM((SB_NPAIR, tm, LANES), F32),
        ],
        compiler_params=pltpu.CompilerParams(
            dimension_semantics=("arbitrary", "arbitrary"), vmem_limit_bytes=VMEM_LIMIT),
        name="mixer",
    )(x, kt, v, g_mix, q_norm_g, conv_w, w_in, w_conv_out, w_x_out, *r_in)
    return outs[0], outs[1], outs[2], [o.reshape(w.shape) for o, w in zip(outs[3:], riders)]


def _post_kernel(x_ref, osb_ref, part_ref, gsb_ref, gmlp_ref, wsb_ref, wout_ref, wup_ref, wdn_ref,
                 out_ref):
    y_sb = _dot(osb_ref[...], wsb_ref[...])
    merged = part_ref[...] + gsb_ref[...] * y_sb
    x1 = x_ref[...] + _dot(merged.astype(BF16), wout_ref[...])
    h2 = _rms(x1, gmlp_ref[...]).astype(BF16)
    acc = x1
    for cidx in range(D_FF // FF_CHUNK):
        lo = cidx * FF_CHUNK
        up = jnp.maximum(_dot(h2, wup_ref[:, lo:lo + FF_CHUNK]), 0.0)
        acc = acc + _dot((up * up).astype(BF16), wdn_ref[lo:lo + FF_CHUNK, :])
    out_ref[...] = acc


def _post(x, o_sb, part, g_sb, g_mlp, w_sb_out, w_out, w_up, w_down):
    b, s, d = x.shape
    tm = POST_TM
    tok = lambda width: pl.BlockSpec((None, tm, width), lambda i, j: (i, j, 0))
    return pl.pallas_call(
        _post_kernel,
        grid=(b, s // tm),
        in_specs=[
            tok(d), tok(SB_DIM), tok(d), tok(d),
            _resident((1, d)),
            _resident((SB_DIM, d)),
            _resident((d, d)),
            _resident((d, D_FF)),
            _resident((D_FF, d)),
        ],
        out_specs=tok(d),
        out_shape=jax.ShapeDtypeStruct((b, s, d), F32),
        compiler_params=pltpu.CompilerParams(
            dimension_semantics=("arbitrary", "arbitrary"), vmem_limit_bytes=VMEM_LIMIT),
        name="post",
    )(x, o_sb, part, g_sb, g_mlp, w_sb_out, w_out, w_up, w_down)


def kernel(x, mem, g_mix, g_mem, w_in, conv_w, w_conv_out, w_sb_out, q_norm_g, k_norm_g,
           w_mem_kv, w_x_out, w_out, g_mlp, w_up, w_down):
    depth = w_in.shape[0]
    for l in range(depth):
        row = lambda g: g[l][None, :]
        kt, v, mixer_w = _mem_kv(mem, row(g_mem), row(k_norm_g), w_mem_kv[l],
                                 (w_in[l], w_conv_out[l], w_x_out[l]))
        o_sb, part, g_sb, post_w = _mixer(
            x, kt, v, row(g_mix), row(q_norm_g), conv_w[l], *mixer_w,
            (w_sb_out[l], w_out[l], w_up[l], w_down[l]))
        x = _post(x, o_sb, part, g_sb, row(g_mlp), *post_w)
    return x
```

```python
import functools

import jax
import jax.numpy as jnp
from jax import lax
from jax.experimental import pallas as pl
from jax.experimental.pallas import tpu as pltpu

D_MODEL = 1024
CONV_DIM = D_MODEL
CONV_WIDTH = 3
SB_HEADS = 16
SB_HEAD_DIM = 64
SB_DIM = SB_HEADS * SB_HEAD_DIM
X_HEADS = 4
X_HEAD_DIM = 256
X_DIM = X_HEADS * X_HEAD_DIM
N_BRANCH = 3
D_FF = 4 * D_MODEL
EPS = 1e-6

OFF_CH = 0
OFF_CB = OFF_CH + CONV_DIM
OFF_CC = OFF_CB + CONV_DIM
OFF_SQ = OFF_CC + CONV_DIM
OFF_SK = OFF_SQ + SB_DIM
OFF_SV = OFF_SK + SB_DIM
OFF_XQ = OFF_SV + SB_DIM
OFF_GATE = OFF_XQ + X_DIM
IN_COLS = OFF_GATE + N_BRANCH * D_MODEL

LANES = 128
SUBLANES = 8
VMEM_LIMIT = 60 * 1024 * 1024

SB_T = 128
SB_R = 2
MIX_TM = SB_R * SB_T
POST_TM = 512
FF_CHUNK = 1024
SB_NPAIR = SB_DIM // LANES
SB_GROUP = 2
SB_PAD = 2
SB_EARLY = 32
HEADS_PER_PAIR = LANES // SB_HEAD_DIM
LOG2E = 1.4426950408889634
SB_K_SCALE = SB_HEAD_DIM ** -0.5 * LOG2E
SB_DEAD = -136.0
SB_Z_MAX = 126.0

BF16 = jnp.bfloat16
F32 = jnp.float32


def _dot(a, b):
    return jnp.dot(a, b, preferred_element_type=F32)


def _rms(x, g):
    return x * lax.rsqrt(jnp.mean(x * x, axis=-1, keepdims=True) + EPS) * g


def _resident(shape):
    return pl.BlockSpec(shape, lambda *_: (0,) * len(shape), pipeline_mode=pl.Buffered(1))


def _lanes(p):
    return slice(p * LANES, (p + 1) * LANES)


def _cast_rider(weights, steps, step_of):
    ins, in_specs, out_specs, out_shapes = [], [], [], []
    for w in weights:
        rows, cols = w.shape
        assert rows % steps == 0, (rows, steps)
        spec = pl.BlockSpec((None, rows // steps, cols), lambda *idx: (step_of(*idx), 0, 0))
        ins.append(w.reshape(steps, rows // steps, cols))
        in_specs.append(spec)
        out_specs.append(spec)
        out_shapes.append(jax.ShapeDtypeStruct((steps, rows // steps, cols), BF16))
    return ins, in_specs, out_specs, out_shapes


def _cast_refs(srcs, dsts):
    for src, dst in zip(srcs, dsts):
        dst[...] = src[...].astype(dst.dtype)


N_MEM_RIDERS = 3


def _mem_kv_kernel(mem_ref, gmem_ref, kg_ref, wkv_ref, *refs):
    cast_in, (kt_ref, v_ref), cast_out = refs[:N_MEM_RIDERS], refs[N_MEM_RIDERS:N_MEM_RIDERS + 2], refs[N_MEM_RIDERS + 2:]
    _cast_refs(cast_in, cast_out)
    mn = _rms(mem_ref[...], gmem_ref[...]).astype(BF16)
    kv = _dot(mn, wkv_ref[...].astype(BF16))
    for hh in range(X_HEADS):
        lo = hh * X_HEAD_DIM
        kn = _rms(kv[:, lo:lo + X_HEAD_DIM], kg_ref[...])
        kt_ref[lo:lo + X_HEAD_DIM, :] = kn.T.astype(BF16)
    v_ref[...] = kv[:, X_DIM:].astype(BF16)


def _mem_kv(mem, g_mem, k_norm_g, w_mem_kv, riders):
    b, m, d = mem.shape
    assert len(riders) == N_MEM_RIDERS
    r_in, r_in_specs, r_out_specs, r_out_shapes = _cast_rider(riders, b, lambda i: i)
    outs = pl.pallas_call(
        _mem_kv_kernel,
        grid=(b,),
        in_specs=[
            pl.BlockSpec((None, m, d), lambda i: (i, 0, 0)),
            _resident((1, d)),
            _resident((1, X_HEAD_DIM)),
            _resident((d, 2 * X_DIM)),
        ] + r_in_specs,
        out_specs=[
            pl.BlockSpec((None, X_DIM, m), lambda i: (i, 0, 0)),
            pl.BlockSpec((None, m, X_DIM), lambda i: (i, 0, 0)),
        ] + r_out_specs,
        out_shape=[
            jax.ShapeDtypeStruct((b, X_DIM, m), BF16),
            jax.ShapeDtypeStruct((b, m, X_DIM), BF16),
        ] + r_out_shapes,
        compiler_params=pltpu.CompilerParams(
            dimension_semantics=("arbitrary",), vmem_limit_bytes=VMEM_LIMIT),
        name="mem_kv",
    )(mem, g_mem, k_norm_g, w_mem_kv, *r_in)
    return outs[0], outs[1], [o.reshape(w.shape) for o, w in zip(outs[2:], riders)]


N_MIX_RIDERS = 4


def _mixer_kernel(x_ref, kt_ref, v_ref, gmix_ref, qg_ref, convw_ref, win_ref, wco_ref, wxo_ref, *refs):
    cast_in, refs = refs[:N_MIX_RIDERS], refs[N_MIX_RIDERS:]
    (osb_ref, part_ref, gsb_ref), refs = refs[:3], refs[3:]
    cast_out, (ubuf, q_scr, kth, vh, tri_ref, carry_ref, acc_ref) = refs[:N_MIX_RIDERS], refs[N_MIX_RIDERS:]
    _cast_refs(cast_in, cast_out)
    t = SB_T
    tq = MIX_TM
    w2 = HEADS_PER_PAIR * t
    tile = pl.program_id(1)
    ja = 2 * tile

    @pl.when(tile == 0)
    def _():
        ubuf[0:SUBLANES, :] = jnp.zeros((SUBLANES, CONV_DIM), F32)
        for p in range(SB_NPAIR):
            kth[p, 0:SB_PAD] = jnp.zeros((SB_PAD, LANES, t), BF16)
            vh[p, 0:SB_PAD] = jnp.zeros((SB_PAD, t, LANES), BF16)

    h = _rms(x_ref[...], gmix_ref[...]).astype(BF16)

    def proj(off, width):
        return _dot(h, win_ref[:, off:off + width])

    q_scr[...] = proj(OFF_SQ, SB_DIM).astype(BF16)
    k_new = proj(OFF_SK, SB_DIM) * SB_K_SCALE
    v_new = proj(OFF_SV, SB_DIM).astype(BF16)
    for p in range(SB_NPAIR):
        for blk in range(SB_R):
            rows = slice(blk * t, (blk + 1) * t)
            kth[p, ja + SB_PAD + blk] = k_new[rows, _lanes(p)].T.astype(BF16)
            vh[p, ja + SB_PAD + blk] = v_new[rows, _lanes(p)]

    r = lax.broadcasted_iota(jnp.int32, (w2, w2), 0) % t
    c = lax.broadcasted_iota(jnp.int32, (w2, w2), 1)
    tri_ref[...] = jnp.where((c >= t) | (r >= c), -1.0, 0.0).astype(BF16)

    row = lax.broadcasted_iota(jnp.int32, (t, w2), 0)
    col = lax.broadcasted_iota(jnp.int32, (t, w2), 1) % t
    past = col < row

    feat_row = lax.broadcasted_iota(jnp.int32, (LANES, t), 0)
    feat_lane = lax.broadcasted_iota(jnp.int32, (t, LANES), 1)
    head_rows = [jnp.where(feat_row // SB_HEAD_DIM == hh, 1.0, 0.0).astype(BF16)
                 for hh in range(HEADS_PER_PAIR)]
    head_lanes = [jnp.where(feat_lane // SB_HEAD_DIM == hh, 1.0, 0.0).astype(BF16)
                  for hh in range(HEADS_PER_PAIR)]

    def mask_diag(x):
        head = jnp.where(past, x[:t], 0.0)
        return head if x.shape[0] == t else jnp.concatenate([head, x[t:]], axis=0)

    def put(full, r0, r1, new):
        parts = ([full[:r0]] if r0 > 0 else []) + [new] + ([full[r1:]] if r1 < full.shape[0] else [])
        return new if len(parts) == 1 else jnp.concatenate(parts, axis=0)

    def tiles_stages(tiles, carry, acc):
        carry, acc = dict(carry), dict(acc)
        zs, lhs = [], []
        for p, r0, r1, j, diag in tiles:
            kt = kth[p, j + SB_PAD]
            kt2 = jnp.concatenate([kt * m for m in head_rows], axis=1)
            zs.append(_dot(q_scr[r0:r1, _lanes(p)], kt2))
        yield None
        for z, (p, r0, r1, j, diag) in zip(zs, tiles):
            sp = jnp.maximum(z, jnp.log(1.0 + jnp.exp2(jnp.minimum(z, SB_Z_MAX))) * LOG2E)
            if diag:
                sp = mask_diag(sp)
            hi = sp.astype(BF16)
            lo = (sp - hi.astype(F32)).astype(BF16)
            lhs.append([jnp.concatenate([hi[:, hh * t:(hh + 1) * t], lo[:, hh * t:(hh + 1) * t]], axis=1)
                        for hh in range(HEADS_PER_PAIR)])
        res = [[_dot(x, tri_ref[...]) for x in xs] for xs in lhs]
        yield None
        avs = []
        for z, rs, (p, r0, r1, j, diag) in zip(zs, res, tiles):
            below = jnp.concatenate([x[:, :t] for x in rs], axis=1)
            total = jnp.concatenate([x[:, t:] for x in rs], axis=1)
            a = jnp.exp2(z + below + carry[p][r0:r1])
            if diag:
                a = mask_diag(a)
            avs.append(a.astype(BF16))
            carry[p] = put(carry[p], r0, r1, carry[p][r0:r1] + total)
        for a, (p, r0, r1, j, diag) in zip(avs, tiles):
            vb = vh[p, j + SB_PAD]
            v2 = jnp.concatenate([vb * m for m in head_lanes], axis=0)
            acc[p] = put(acc[p], r0, r1, acc[p][r0:r1] + _dot(a, v2))
        yield carry, acc

    early = SB_EARLY
    steps = [(t, tq, ja + 1, True), (0, tq, ja, True), (0, t + early, ja - 1, False), (0, early, ja - 2, False)]
    groups = [range(g, g + SB_GROUP) for g in range(0, SB_NPAIR, SB_GROUP)]
    sweeps = [tiles_stages([(p,) + st for st in steps for p in grp],
                           {p: jnp.zeros((tq, w2), F32) for p in grp},
                           {p: jnp.zeros((tq, LANES), F32) for p in grp}) for grp in groups]

    def token_local():
        u = proj(OFF_CC, CONV_DIM) * proj(OFF_CH, CONV_DIM)
        ubuf[SUBLANES:SUBLANES + tq, :] = u
        cw = convw_ref[...]
        conv = (cw[0:1, :] * ubuf[SUBLANES - 2:SUBLANES - 2 + tq, :]
                + cw[1:2, :] * ubuf[SUBLANES - 1:SUBLANES - 1 + tq, :]
                + cw[2:3, :] * u)
        ubuf[0:SUBLANES, :] = ubuf[tq:tq + SUBLANES, :]
        yield
        y_conv = _dot((proj(OFF_CB, CONV_DIM) * conv).astype(BF16), wco_ref[...])
        yield
        xq = proj(OFF_XQ, X_DIM)
        heads = []
        for hh in range(X_HEADS):
            lo = hh * X_HEAD_DIM
            qn = _rms(xq[:, lo:lo + X_HEAD_DIM], qg_ref[...]).astype(BF16)
            s = _dot(qn, kt_ref[lo:lo + X_HEAD_DIM, :]) * (X_HEAD_DIM ** -0.5)
            pr = jnp.exp(s - jnp.max(s, axis=-1, keepdims=True))
            l = jnp.sum(pr, axis=-1, keepdims=True)
            heads.append(_dot(pr.astype(BF16), v_ref[:, lo:lo + X_HEAD_DIM]) / l)
            if hh % 2 == 1:
                yield
        y_x = _dot(jnp.concatenate(heads, axis=-1).astype(BF16), wxo_ref[...])
        yield
        g_conv = jax.nn.sigmoid(proj(OFF_GATE, D_MODEL))
        yield
        g_x = jax.nn.sigmoid(proj(OFF_GATE + 2 * D_MODEL, D_MODEL))
        part_ref[...] = (g_conv * y_conv + g_x * y_x).astype(part_ref.dtype)
        yield
        gsb_ref[...] = jax.nn.sigmoid(proj(OFF_GATE + D_MODEL, D_MODEL)).astype(gsb_ref.dtype)

    local = token_local()
    pieces_per_stage = SB_GROUP // 2
    state = []
    for sweep in sweeps:
        next(sweep)
        for _ in range(pieces_per_stage):
            next(local, None)
        next(sweep)
        for _ in range(pieces_per_stage):
            next(local, None)
        state.append(next(sweep))
    for _ in local:
        pass

    carry = {p: c for cs, _ in state for p, c in cs.items()}
    acc = {p: a for _, accs in state for p, a in accs.items()}
    for p in range(SB_NPAIR):
        carry_ref[p] = carry[p]
        acc_ref[p] = acc[p]
        osb_ref[:, _lanes(p)] = acc[p].astype(osb_ref.dtype)

    def alive(rows):
        return functools.reduce(jnp.maximum, [jnp.max(x) for x in rows]) >= SB_DEAD

    rest = ((0, early, ja - 3), (early, t + early, ja - 2), (t + early, tq, ja - 1))
    alive0 = [alive([carry[p][r0:r1] for p in range(SB_NPAIR)]) for r0, r1, _ in rest]
    for (r0, r1, j_next), go in zip(rest, alive0):
        def k_cond(st):
            return jnp.logical_and(st[0] >= 0, st[1])

        def k_body(st, r0=r0, r1=r1):
            *_, (cnew, anew) = tiles_stages([(p, r0, r1, st[0], False) for p in range(SB_NPAIR)],
                                            {p: carry_ref[p] for p in range(SB_NPAIR)},
                                            {p: acc_ref[p] for p in range(SB_NPAIR)})
            for p in range(SB_NPAIR):
                carry_ref[p] = cnew[p]
                acc_ref[p] = anew[p]
                osb_ref[r0:r1, _lanes(p)] = anew[p][r0:r1].astype(osb_ref.dtype)
            return st[0] - 1, alive([cnew[p][r0:r1] for p in range(SB_NPAIR)])

        lax.while_loop(k_cond, k_body, (j_next, go))


def _mixer(x, kt, v, g_mix, q_norm_g, conv_w, w_in, w_conv_out, w_x_out, riders):
    b, s, d = x.shape
    m = v.shape[1]
    tm = MIX_TM
    nt = s // tm
    nblk = s // SB_T
    w2 = HEADS_PER_PAIR * SB_T
    assert len(riders) == N_MIX_RIDERS
    r_in, r_in_specs, r_out_specs, r_out_shapes = _cast_rider(riders, b * nt, lambda i, j: i * nt + j)
    tok = lambda width: pl.BlockSpec((None, tm, width), lambda i, j: (i, j, 0))
    outs = pl.pallas_call(
        _mixer_kernel,
        grid=(b, s // tm),
        in_specs=[
            tok(d),
            pl.BlockSpec((None, X_DIM, m), lambda i, j: (i, 0, 0), pipeline_mode=pl.Buffered(1)),
            pl.BlockSpec((None, m, X_DIM), lambda i, j: (i, 0, 0), pipeline_mode=pl.Buffered(1)),
            _resident((1, d)),
            _resident((1, X_HEAD_DIM)),
            _resident((CONV_WIDTH, CONV_DIM)),
            _resident((d, IN_COLS)),
            _resident((CONV_DIM, d)),
            _resident((X_DIM, d)),
        ] + r_in_specs,
        out_specs=[tok(SB_DIM), tok(d), tok(d)] + r_out_specs,
        out_shape=[
            jax.ShapeDtypeStruct((b, s, SB_DIM), BF16),
            jax.ShapeDtypeStruct((b, s, d), BF16),
            jax.ShapeDtypeStruct((b, s, d), BF16),
        ] + r_out_shapes,
        scratch_shapes=[
            pltpu.VMEM((tm + SUBLANES, CONV_DIM), F32),
            pltpu.VMEM((tm, SB_DIM), BF16),
            pltpu.VMEM((SB_NPAIR, nblk + SB_PAD, LANES, SB_T), BF16),
            pltpu.VMEM((SB_NPAIR, nblk + SB_PAD, SB_T, LANES), BF16),
            pltpu.VMEM((w2, w2), BF16),
            pltpu.VMEM((SB_NPAIR, tm, w2), F32),
            pltpu.VMEM((SB_NPAIR, tm, LANES), F32),
        ],
        compiler_params=pltpu.CompilerParams(
            dimension_semantics=("arbitrary", "arbitrary"), vmem_limit_bytes=VMEM_LIMIT),
        name="mixer",
    )(x, kt, v, g_mix, q_norm_g, conv_w, w_in, w_conv_out, w_x_out, *r_in)
    return outs[0], outs[1], outs[2], [o.reshape(w.shape) for o, w in zip(outs[3:], riders)]


def _post_kernel(x_ref, osb_ref, part_ref, gsb_ref, gmlp_ref, wsb_ref, wout_ref, wup_ref, wdn_ref,
                 out_ref):
    y_sb = _dot(osb_ref[...], wsb_ref[...])
    merged = part_ref[...] + gsb_ref[...] * y_sb
    x1 = x_ref[...] + _dot(merged.astype(BF16), wout_ref[...])
    h2 = _rms(x1, gmlp_ref[...]).astype(BF16)
    acc = x1
    for cidx in range(D_FF // FF_CHUNK):
        lo = cidx * FF_CHUNK
        up = jnp.maximum(_dot(h2, wup_ref[:, lo:lo + FF_CHUNK]), 0.0)
        acc = acc + _dot((up * up).astype(BF16), wdn_ref[lo:lo + FF_CHUNK, :])
    out_ref[...] = acc


def _post(x, o_sb, part, g_sb, g_mlp, w_sb_out, w_out, w_up, w_down):
    b, s, d = x.shape
    tm = POST_TM
    tok = lambda width: pl.BlockSpec((None, tm, width), lambda i, j: (i, j, 0))
    return pl.pallas_call(
        _post_kernel,
        grid=(b, s // tm),
        in_specs=[
            tok(d), tok(SB_DIM), tok(d), tok(d),
            _resident((1, d)),
            _resident((SB_DIM, d)),
            _resident((d, d)),
            _resident((d, D_FF)),
            _resident((D_FF, d)),
        ],
        out_specs=tok(d),
        out_shape=jax.ShapeDtypeStruct((b, s, d), F32),
        compiler_params=pltpu.CompilerParams(
            dimension_semantics=("arbitrary", "arbitrary"), vmem_limit_bytes=VMEM_LIMIT),
        name="post",
    )(x, o_sb, part, g_sb, g_mlp, w_sb_out, w_out, w_up, w_down)


def kernel(x, mem, g_mix, g_mem, w_in, conv_w, w_conv_out, w_sb_out, q_norm_g, k_norm_g,
           w_mem_kv, w_x_out, w_out, g_mlp, w_up, w_down):
    depth = w_in.shape[0]
    for l in range(depth):
        row = lambda g: g[l][None, :]
        kt, v, mixer_w = _mem_kv(mem, row(g_mem), row(k_norm_g), w_mem_kv[l],
                                 (w_in[l], w_conv_out[l], w_x_out[l]))
        o_sb, part, g_sb, post_w = _mixer(
            x, kt, v, row(g_mix), row(q_norm_g), conv_w[l], *mixer_w,
            (w_sb_out[l], w_out[l], w_up[l], w_down[l]))
        x = _post(x, o_sb, part, g_sb, row(g_mlp), *post_w)
    return x
```

```python
import functools

import jax
import jax.numpy as jnp
from jax import lax
from jax.experimental import pallas as pl
from jax.experimental.pallas import tpu as pltpu

D_MODEL = 1024
CONV_DIM = D_MODEL
CONV_WIDTH = 3
SB_HEADS = 16
SB_HEAD_DIM = 64
SB_DIM = SB_HEADS * SB_HEAD_DIM
X_HEADS = 4
X_HEAD_DIM = 256
X_DIM = X_HEADS * X_HEAD_DIM
N_BRANCH = 3
D_FF = 4 * D_MODEL
EPS = 1e-6

OFF_CH = 0
OFF_CB = OFF_CH + CONV_DIM
OFF_CC = OFF_CB + CONV_DIM
OFF_SQ = OFF_CC + CONV_DIM
OFF_SK = OFF_SQ + SB_DIM
OFF_SV = OFF_SK + SB_DIM
OFF_XQ = OFF_SV + SB_DIM
OFF_GATE = OFF_XQ + X_DIM
IN_COLS = OFF_GATE + N_BRANCH * D_MODEL

LANES = 128
SUBLANES = 8
VMEM_LIMIT = 60 * 1024 * 1024

SB_T = 128
SB_R = 2
MIX_TM = SB_R * SB_T
POST_TM = 512
FF_CHUNK = 1024
SB_NPAIR = SB_DIM // LANES
SB_GROUP = 2
SB_PAD = 2
SB_EARLY = 48
HEADS_PER_PAIR = LANES // SB_HEAD_DIM
LOG2E = 1.4426950408889634
SB_K_SCALE = SB_HEAD_DIM ** -0.5 * LOG2E
SB_DEAD = -152.0
SB_Z_MAX = 126.0

BF16 = jnp.bfloat16
F32 = jnp.float32


def _dot(a, b):
    return jnp.dot(a, b, preferred_element_type=F32)


def _rms(x, g):
    return x * lax.rsqrt(jnp.mean(x * x, axis=-1, keepdims=True) + EPS) * g


def _resident(shape):
    return pl.BlockSpec(shape, lambda *_: (0,) * len(shape), pipeline_mode=pl.Buffered(1))


def _lanes(p):
    return slice(p * LANES, (p + 1) * LANES)


def _cast_rider(weights, steps, step_of):
    ins, in_specs, out_specs, out_shapes = [], [], [], []
    for w in weights:
        rows, cols = w.shape
        assert rows % steps == 0, (rows, steps)
        spec = pl.BlockSpec((None, rows // steps, cols), lambda *idx: (step_of(*idx), 0, 0))
        ins.append(w.reshape(steps, rows // steps, cols))
        in_specs.append(spec)
        out_specs.append(spec)
        out_shapes.append(jax.ShapeDtypeStruct((steps, rows // steps, cols), BF16))
    return ins, in_specs, out_specs, out_shapes


def _cast_refs(srcs, dsts):
    for src, dst in zip(srcs, dsts):
        dst[...] = src[...].astype(dst.dtype)


N_MEM_RIDERS = 3


MEM_RING = 3


def _mem_kv_kernel(mem_ref, gmem_ref, kg_ref, wkv_ref, win_hbm, *refs):
    n_small = N_MEM_RIDERS - 1
    cast_in, refs = refs[:n_small], refs[n_small:]
    (kt_ref, v_ref, win_out), refs = refs[:3], refs[3:]
    cast_out, (ring, sem) = refs[:n_small], refs[n_small:]
    step = pl.program_id(0)
    steps = pl.num_programs(0)

    def chunk_copy(k):
        slot = k % MEM_RING
        return pltpu.make_async_copy(win_hbm.at[k], ring.at[slot], sem.at[slot])

    @pl.when(step == 0)
    def _():
        for k in range(MEM_RING - 1):
            chunk_copy(k).start()

    @pl.when(step + (MEM_RING - 1) < steps)
    def _():
        chunk_copy(step + (MEM_RING - 1)).start()

    _cast_refs(cast_in, cast_out)
    mn = _rms(mem_ref[...], gmem_ref[...]).astype(BF16)
    kv = _dot(mn, wkv_ref[...].astype(BF16))
    for hh in range(X_HEADS):
        lo = hh * X_HEAD_DIM
        kn = _rms(kv[:, lo:lo + X_HEAD_DIM], kg_ref[...])
        kt_ref[lo:lo + X_HEAD_DIM, :] = kn.T.astype(BF16)
    v_ref[...] = kv[:, X_DIM:].astype(BF16)

    chunk_copy(step).wait()
    win_out[...] = ring[step % MEM_RING].astype(win_out.dtype)


def _mem_kv(mem, g_mem, k_norm_g, w_mem_kv, riders):
    b, m, d = mem.shape
    assert len(riders) == N_MEM_RIDERS and b >= MEM_RING - 1
    r_in, r_in_specs, r_out_specs, r_out_shapes = _cast_rider(riders, b, lambda i: i)
    r_in_specs[0] = pl.BlockSpec(memory_space=pl.ANY)
    chunk = r_in[0].shape[1:]
    outs = pl.pallas_call(
        _mem_kv_kernel,
        grid=(b,),
        in_specs=[
            pl.BlockSpec((None, m, d), lambda i: (i, 0, 0)),
            _resident((1, d)),
            _resident((1, X_HEAD_DIM)),
            _resident((d, 2 * X_DIM)),
        ] + r_in_specs,
        out_specs=[
            pl.BlockSpec((None, X_DIM, m), lambda i: (i, 0, 0)),
            pl.BlockSpec((None, m, X_DIM), lambda i: (i, 0, 0)),
        ] + r_out_specs,
        out_shape=[
            jax.ShapeDtypeStruct((b, X_DIM, m), BF16),
            jax.ShapeDtypeStruct((b, m, X_DIM), BF16),
        ] + r_out_shapes,
        scratch_shapes=[
            pltpu.VMEM((MEM_RING,) + chunk, F32),
            pltpu.SemaphoreType.DMA((MEM_RING,)),
        ],
        compiler_params=pltpu.CompilerParams(
            dimension_semantics=("arbitrary",), vmem_limit_bytes=VMEM_LIMIT),
        name="mem_kv",
    )(mem, g_mem, k_norm_g, w_mem_kv, *r_in)
    return outs[0], outs[1], [o.reshape(w.shape) for o, w in zip(outs[2:], riders)]


N_MIX_RIDERS = 4


def _mixer_kernel(x_ref, kt_ref, v_ref, gmix_ref, qg_ref, convw_ref, win_ref, wco_ref, wxo_ref, *refs):
    cast_in, refs = refs[:N_MIX_RIDERS], refs[N_MIX_RIDERS:]
    (osb_ref, part_ref, gsb_ref), refs = refs[:3], refs[3:]
    cast_out, (ubuf, q_scr, kth, vh, tri_ref, carry_ref, acc_ref) = refs[:N_MIX_RIDERS], refs[N_MIX_RIDERS:]
    _cast_refs(cast_in, cast_out)
    t = SB_T
    tq = MIX_TM
    w2 = HEADS_PER_PAIR * t
    tile = pl.program_id(1)
    ja = 2 * tile

    @pl.when(tile == 0)
    def _():
        ubuf[0:SUBLANES, :] = jnp.zeros((SUBLANES, CONV_DIM), F32)
        for p in range(SB_NPAIR):
            kth[p, 0:SB_PAD] = jnp.zeros((SB_PAD, LANES, t), BF16)
            vh[p, 0:SB_PAD] = jnp.zeros((SB_PAD, t, LANES), BF16)

    h = _rms(x_ref[...], gmix_ref[...]).astype(BF16)

    def proj(off, width):
        return _dot(h, win_ref[:, off:off + width])

    q_scr[...] = proj(OFF_SQ, SB_DIM).astype(BF16)
    k_new = proj(OFF_SK, SB_DIM) * SB_K_SCALE
    v_new = proj(OFF_SV, SB_DIM).astype(BF16)
    for p in range(SB_NPAIR):
        for blk in range(SB_R):
            rows = slice(blk * t, (blk + 1) * t)
            kth[p, ja + SB_PAD + blk] = k_new[rows, _lanes(p)].T.astype(BF16)
            vh[p, ja + SB_PAD + blk] = v_new[rows, _lanes(p)]

    r = lax.broadcasted_iota(jnp.int32, (w2, w2), 0) % t
    c = lax.broadcasted_iota(jnp.int32, (w2, w2), 1)
    tri_ref[...] = jnp.where((c >= t) | (r >= c), -1.0, 0.0).astype(BF16)

    row = lax.broadcasted_iota(jnp.int32, (t, w2), 0)
    col = lax.broadcasted_iota(jnp.int32, (t, w2), 1) % t
    past = col < row

    feat_row = lax.broadcasted_iota(jnp.int32, (LANES, t), 0)
    feat_lane = lax.broadcasted_iota(jnp.int32, (t, LANES), 1)
    head_rows = [jnp.where(feat_row // SB_HEAD_DIM == hh, 1.0, 0.0).astype(BF16)
                 for hh in range(HEADS_PER_PAIR)]
    head_lanes = [jnp.where(feat_lane // SB_HEAD_DIM == hh, 1.0, 0.0).astype(BF16)
                  for hh in range(HEADS_PER_PAIR)]

    def mask_diag(x):
        head = jnp.where(past, x[:t], 0.0)
        return head if x.shape[0] == t else jnp.concatenate([head, x[t:]], axis=0)

    def put(full, r0, r1, new):
        parts = ([full[:r0]] if r0 > 0 else []) + [new] + ([full[r1:]] if r1 < full.shape[0] else [])
        return new if len(parts) == 1 else jnp.concatenate(parts, axis=0)

    def tiles_stages(tiles, carry, acc):
        carry, acc = dict(carry), dict(acc)
        zs, lhs = [], []
        for p, r0, r1, j, diag in tiles:
            kt = kth[p, j + SB_PAD]
            kt2 = jnp.concatenate([kt * m for m in head_rows], axis=1)
            zs.append(_dot(q_scr[r0:r1, _lanes(p)], kt2))
        yield None
        for z, (p, r0, r1, j, diag) in zip(zs, tiles):
            sp = jnp.maximum(z, jnp.log(1.0 + jnp.exp2(jnp.minimum(z, SB_Z_MAX))) * LOG2E)
            if diag:
                sp = mask_diag(sp)
            hi = sp.astype(BF16)
            lo = (sp - hi.astype(F32)).astype(BF16)
            lhs.append([jnp.concatenate([hi[:, hh * t:(hh + 1) * t], lo[:, hh * t:(hh + 1) * t]], axis=1)
                        for hh in range(HEADS_PER_PAIR)])
        res = [[_dot(x, tri_ref[...]) for x in xs] for xs in lhs]
        yield None
        avs = []
        for z, rs, (p, r0, r1, j, diag) in zip(zs, res, tiles):
            below = jnp.concatenate([x[:, :t] for x in rs], axis=1)
            total = jnp.concatenate([x[:, t:] for x in rs], axis=1)
            a = jnp.exp2(z + below + carry[p][r0:r1])
            if diag:
                a = mask_diag(a)
            avs.append(a.astype(BF16))
            carry[p] = put(carry[p], r0, r1, carry[p][r0:r1] + total)
        for a, (p, r0, r1, j, diag) in zip(avs, tiles):
            vb = vh[p, j + SB_PAD]
            v2 = jnp.concatenate([vb * m for m in head_lanes], axis=0)
            acc[p] = put(acc[p], r0, r1, acc[p][r0:r1] + _dot(a, v2))
        yield carry, acc

    early = SB_EARLY
    steps = [(t, tq, ja + 1, True), (0, tq, ja, True), (0, t + early, ja - 1, False), (0, early, ja - 2, False)]
    groups = [range(g, g + SB_GROUP) for g in range(0, SB_NPAIR, SB_GROUP)]
    sweeps = [tiles_stages([(p,) + st for st in steps for p in grp],
                           {p: jnp.zeros((tq, w2), F32) for p in grp},
                           {p: jnp.zeros((tq, LANES), F32) for p in grp}) for grp in groups]

    def token_local():
        u = proj(OFF_CC, CONV_DIM) * proj(OFF_CH, CONV_DIM)
        ubuf[SUBLANES:SUBLANES + tq, :] = u
        cw = convw_ref[...]
        conv = (cw[0:1, :] * ubuf[SUBLANES - 2:SUBLANES - 2 + tq, :]
                + cw[1:2, :] * ubuf[SUBLANES - 1:SUBLANES - 1 + tq, :]
                + cw[2:3, :] * u)
        ubuf[0:SUBLANES, :] = ubuf[tq:tq + SUBLANES, :]
        yield
        y_conv = _dot((proj(OFF_CB, CONV_DIM) * conv).astype(BF16), wco_ref[...])
        yield
        xq = proj(OFF_XQ, X_DIM)
        heads = []
        for hh in range(X_HEADS):
            lo = hh * X_HEAD_DIM
            qn = _rms(xq[:, lo:lo + X_HEAD_DIM], qg_ref[...]).astype(BF16)
            s = _dot(qn, kt_ref[lo:lo + X_HEAD_DIM, :]) * (X_HEAD_DIM ** -0.5)
            pr = jnp.exp(s - jnp.max(s, axis=-1, keepdims=True))
            l = jnp.sum(pr, axis=-1, keepdims=True)
            heads.append(_dot(pr.astype(BF16), v_ref[:, lo:lo + X_HEAD_DIM]) / l)
            if hh % 2 == 1:
                yield
        y_x = _dot(jnp.concatenate(heads, axis=-1).astype(BF16), wxo_ref[...])
        yield
        g_conv = jax.nn.sigmoid(proj(OFF_GATE, D_MODEL))
        yield
        g_x = jax.nn.sigmoid(proj(OFF_GATE + 2 * D_MODEL, D_MODEL))
        part_ref[...] = (g_conv * y_conv + g_x * y_x).astype(part_ref.dtype)
        yield
        gsb_ref[...] = jax.nn.sigmoid(proj(OFF_GATE + D_MODEL, D_MODEL)).astype(gsb_ref.dtype)

    local = token_local()
    pieces_per_stage = SB_GROUP // 2
    state = []
    for sweep in sweeps:
        next(sweep)
        for _ in range(pieces_per_stage):
            next(local, None)
        next(sweep)
        for _ in range(pieces_per_stage):
            next(local, None)
        state.append(next(sweep))
    for _ in local:
        pass

    carry = {p: c for cs, _ in state for p, c in cs.items()}
    acc = {p: a for _, accs in state for p, a in accs.items()}
    for p in range(SB_NPAIR):
        carry_ref[p] = carry[p]
        acc_ref[p] = acc[p]
        osb_ref[:, _lanes(p)] = acc[p].astype(osb_ref.dtype)

    def alive(rows):
        return functools.reduce(jnp.maximum, [jnp.max(x) for x in rows]) >= SB_DEAD

    rest = ((0, early, ja - 3), (early, t + early, ja - 2), (t + early, tq, ja - 1))
    alive0 = [alive([carry[p][r0:r1] for p in range(SB_NPAIR)]) for r0, r1, _ in rest]
    for (r0, r1, j_next), go in zip(rest, alive0):
        def k_cond(st):
            return jnp.logical_and(st[0] >= 0, st[1])

        def k_body(st, r0=r0, r1=r1):
            *_, (cnew, anew) = tiles_stages([(p, r0, r1, st[0], False) for p in range(SB_NPAIR)],
                                            {p: carry_ref[p] for p in range(SB_NPAIR)},
                                            {p: acc_ref[p] for p in range(SB_NPAIR)})
            for p in range(SB_NPAIR):
                carry_ref[p] = cnew[p]
                acc_ref[p] = anew[p]
                osb_ref[r0:r1, _lanes(p)] = anew[p][r0:r1].astype(osb_ref.dtype)
            return st[0] - 1, alive([cnew[p][r0:r1] for p in range(SB_NPAIR)])

        lax.while_loop(k_cond, k_body, (j_next, go))


def _mixer(x, kt, v, g_mix, q_norm_g, conv_w, w_in, w_conv_out, w_x_out, riders):
    b, s, d = x.shape
    m = v.shape[1]
    tm = MIX_TM
    nt = s // tm
    nblk = s // SB_T
    w2 = HEADS_PER_PAIR * SB_T
    assert len(riders) == N_MIX_RIDERS
    r_in, r_in_specs, r_out_specs, r_out_shapes = _cast_rider(riders, b * nt, lambda i, j: i * nt + j)
    tok = lambda width: pl.BlockSpec((None, tm, width), lambda i, j: (i, j, 0))
    outs = pl.pallas_call(
        _mixer_kernel,
        grid=(b, s // tm),
        in_specs=[
            tok(d),
            pl.BlockSpec((None, X_DIM, m), lambda i, j: (i, 0, 0), pipeline_mode=pl.Buffered(1)),
            pl.BlockSpec((None, m, X_DIM), lambda i, j: (i, 0, 0), pipeline_mode=pl.Buffered(1)),
            _resident((1, d)),
            _resident((1, X_HEAD_DIM)),
            _resident((CONV_WIDTH, CONV_DIM)),
            _resident((d, IN_COLS)),
            _resident((CONV_DIM, d)),
            _resident((X_DIM, d)),
        ] + r_in_specs,
        out_specs=[tok(SB_DIM), tok(d), tok(d)] + r_out_specs,
        out_shape=[
            jax.ShapeDtypeStruct((b, s, SB_DIM), BF16),
            jax.ShapeDtypeStruct((b, s, d), BF16),
            jax.ShapeDtypeStruct((b, s, d), BF16),
        ] + r_out_shapes,
        scratch_shapes=[
            pltpu.VMEM((tm + SUBLANES, CONV_DIM), F32),
            pltpu.VMEM((tm, SB_DIM), BF16),
            pltpu.VMEM((SB_NPAIR, nblk + SB_PAD, LANES, SB_T), BF16),
            pltpu.VMEM((SB_NPAIR, nblk + SB_PAD, SB_T, LANES), BF16),
            pltpu.VMEM((w2, w2), BF16),
            pltpu.VMEM((SB_NPAIR, tm, w2), F32),
            pltpu.VMEM((SB_NPAIR, tm, LANES), F32),
        ],
        compiler_params=pltpu.CompilerParams(
            dimension_semantics=("arbitrary", "arbitrary"), vmem_limit_bytes=VMEM_LIMIT),
        name="mixer",
    )(x, kt, v, g_mix, q_norm_g, conv_w, w_in, w_conv_out, w_x_out, *r_in)
    return outs[0], outs[1], outs[2], [o.reshape(w.shape) for o, w in zip(outs[3:], riders)]


def _post_kernel(x_ref, osb_ref, part_ref, gsb_ref, gmlp_ref, wsb_ref, wout_ref, wup_ref, wdn_ref,
                 out_ref):
    y_sb = _dot(osb_ref[...], wsb_ref[...])
    merged = part_ref[...] + gsb_ref[...] * y_sb
    x1 = x_ref[...] + _dot(merged.astype(BF16), wout_ref[...])
    h2 = _rms(x1, gmlp_ref[...]).astype(BF16)
    acc = x1
    for cidx in range(D_FF // FF_CHUNK):
        lo = cidx * FF_CHUNK
        up = jnp.maximum(_dot(h2, wup_ref[:, lo:lo + FF_CHUNK]), 0.0)
        acc = acc + _dot((up * up).astype(BF16), wdn_ref[lo:lo + FF_CHUNK, :])
    out_ref[...] = acc


def _post(x, o_sb, part, g_sb, g_mlp, w_sb_out, w_out, w_up, w_down):
    b, s, d = x.shape
    tm = POST_TM
    tok = lambda width: pl.BlockSpec((None, tm, width), lambda i, j: (i, j, 0))
    return pl.pallas_call(
        _post_kernel,
        grid=(b, s // tm),
        in_specs=[
            tok(d), tok(SB_DIM), tok(d), tok(d),
            _resident((1, d)),
            _resident((SB_DIM, d)),
            _resident((d, d)),
            _resident((d, D_FF)),
            _resident((D_FF, d)),
        ],
        out_specs=tok(d),
        out_shape=jax.ShapeDtypeStruct((b, s, d), F32),
        compiler_params=pltpu.CompilerParams(
            dimension_semantics=("arbitrary", "arbitrary"), vmem_limit_bytes=VMEM_LIMIT),
        name="post",
    )(x, o_sb, part, g_sb, g_mlp, w_sb_out, w_out, w_up, w_down)


def kernel(x, mem, g_mix, g_mem, w_in, conv_w, w_conv_out, w_sb_out, q_norm_g, k_norm_g,
           w_mem_kv, w_x_out, w_out, g_mlp, w_up, w_down):
    depth = w_in.shape[0]
    for l in range(depth):
        row = lambda g: g[l][None, :]
        kt, v, mixer_w = _mem_kv(mem, row(g_mem), row(k_norm_g), w_mem_kv[l],
                                 (w_in[l], w_conv_out[l], w_x_out[l]))
        o_sb, part, g_sb, post_w = _mixer(
            x, kt, v, row(g_mix), row(q_norm_g), conv_w[l], *mixer_w,
            (w_sb_out[l], w_out[l], w_up[l], w_down[l]))
        x = _post(x, o_sb, part, g_sb, row(g_mlp), *post_w)
    return x
```

```python
import functools

import jax
import jax.numpy as jnp
from jax import lax
from jax.experimental import pallas as pl
from jax.experimental.pallas import tpu as pltpu

D_MODEL = 1024
CONV_DIM = D_MODEL
CONV_WIDTH = 3
SB_HEADS = 16
SB_HEAD_DIM = 64
SB_DIM = SB_HEADS * SB_HEAD_DIM
X_HEADS = 4
X_HEAD_DIM = 256
X_DIM = X_HEADS * X_HEAD_DIM
N_BRANCH = 3
D_FF = 4 * D_MODEL
EPS = 1e-6

OFF_CH = 0
OFF_CB = OFF_CH + CONV_DIM
OFF_CC = OFF_CB + CONV_DIM
OFF_SQ = OFF_CC + CONV_DIM
OFF_SK = OFF_SQ + SB_DIM
OFF_SV = OFF_SK + SB_DIM
OFF_XQ = OFF_SV + SB_DIM
OFF_GATE = OFF_XQ + X_DIM
IN_COLS = OFF_GATE + N_BRANCH * D_MODEL

LANES = 128
SUBLANES = 8
VMEM_LIMIT = 60 * 1024 * 1024

SB_T = 128
SB_R = 2
MIX_TM = SB_R * SB_T
POST_TM = 512
FF_CHUNK = 1024
SB_NPAIR = SB_DIM // LANES
SB_GROUP = 2
SB_PAD = 2
SB_EARLY = 48
HEADS_PER_PAIR = LANES // SB_HEAD_DIM
LOG2E = 1.4426950408889634
SB_K_SCALE = SB_HEAD_DIM ** -0.5 * LOG2E
SB_DEAD = -152.0
SB_Z_MAX = 126.0

BF16 = jnp.bfloat16
F32 = jnp.float32


def _dot(a, b):
    return jnp.dot(a, b, preferred_element_type=F32)


def _rms(x, g):
    return x * lax.rsqrt(jnp.mean(x * x, axis=-1, keepdims=True) + EPS) * g


def _resident(shape):
    return pl.BlockSpec(shape, lambda *_: (0,) * len(shape), pipeline_mode=pl.Buffered(1))


def _lanes(p):
    return slice(p * LANES, (p + 1) * LANES)


def _cast_rider(weights, steps, step_of):
    ins, in_specs, out_specs, out_shapes = [], [], [], []
    for w in weights:
        rows, cols = w.shape
        assert rows % steps == 0, (rows, steps)
        spec = pl.BlockSpec((None, rows // steps, cols), lambda *idx: (step_of(*idx), 0, 0))
        ins.append(w.reshape(steps, rows // steps, cols))
        in_specs.append(spec)
        out_specs.append(spec)
        out_shapes.append(jax.ShapeDtypeStruct((steps, rows // steps, cols), BF16))
    return ins, in_specs, out_specs, out_shapes


def _cast_refs(srcs, dsts):
    for src, dst in zip(srcs, dsts):
        dst[...] = src[...].astype(dst.dtype)


N_MEM_RIDERS = 3


MEM_RING = 4


def _mem_kv_kernel(mem_ref, gmem_ref, kg_ref, wkv_ref, win_hbm, *refs):
    n_small = N_MEM_RIDERS - 1
    cast_in, refs = refs[:n_small], refs[n_small:]
    (kt_ref, v_ref, win_out), refs = refs[:3], refs[3:]
    cast_out, (ring, sem) = refs[:n_small], refs[n_small:]
    step = pl.program_id(0)
    steps = pl.num_programs(0)

    def chunk_copy(k):
        slot = k % MEM_RING
        return pltpu.make_async_copy(win_hbm.at[k], ring.at[slot], sem.at[slot])

    @pl.when(step == 0)
    def _():
        for k in range(MEM_RING - 1):
            chunk_copy(k).start()

    @pl.when(step + (MEM_RING - 1) < steps)
    def _():
        chunk_copy(step + (MEM_RING - 1)).start()

    _cast_refs(cast_in, cast_out)
    mn = _rms(mem_ref[...], gmem_ref[...]).astype(BF16)
    kv = _dot(mn, wkv_ref[...].astype(BF16))
    for hh in range(X_HEADS):
        lo = hh * X_HEAD_DIM
        kn = _rms(kv[:, lo:lo + X_HEAD_DIM], kg_ref[...])
        kt_ref[lo:lo + X_HEAD_DIM, :] = kn.T.astype(BF16)
    v_ref[...] = kv[:, X_DIM:].astype(BF16)

    chunk_copy(step).wait()
    win_out[...] = ring[step % MEM_RING].astype(win_out.dtype)


def _mem_kv(mem, g_mem, k_norm_g, w_mem_kv, riders):
    b, m, d = mem.shape
    assert len(riders) == N_MEM_RIDERS and b >= MEM_RING - 1
    r_in, r_in_specs, r_out_specs, r_out_shapes = _cast_rider(riders, b, lambda i: i)
    r_in_specs[0] = pl.BlockSpec(memory_space=pl.ANY)
    chunk = r_in[0].shape[1:]
    outs = pl.pallas_call(
        _mem_kv_kernel,
        grid=(b,),
        in_specs=[
            pl.BlockSpec((None, m, d), lambda i: (i, 0, 0)),
            _resident((1, d)),
            _resident((1, X_HEAD_DIM)),
            _resident((d, 2 * X_DIM)),
        ] + r_in_specs,
        out_specs=[
            pl.BlockSpec((None, X_DIM, m), lambda i: (i, 0, 0)),
            pl.BlockSpec((None, m, X_DIM), lambda i: (i, 0, 0)),
        ] + r_out_specs,
        out_shape=[
            jax.ShapeDtypeStruct((b, X_DIM, m), BF16),
            jax.ShapeDtypeStruct((b, m, X_DIM), BF16),
        ] + r_out_shapes,
        scratch_shapes=[
            pltpu.VMEM((MEM_RING,) + chunk, F32),
            pltpu.SemaphoreType.DMA((MEM_RING,)),
        ],
        compiler_params=pltpu.CompilerParams(
            dimension_semantics=("arbitrary",), vmem_limit_bytes=VMEM_LIMIT),
        name="mem_kv",
    )(mem, g_mem, k_norm_g, w_mem_kv, *r_in)
    return outs[0], outs[1], [o.reshape(w.shape) for o, w in zip(outs[2:], riders)]


N_MIX_RIDERS = 4


def _mixer_kernel(x_ref, kt_ref, v_ref, gmix_ref, qg_ref, convw_ref, win_ref, wco_ref, wxo_ref, *refs):
    cast_in, refs = refs[:N_MIX_RIDERS], refs[N_MIX_RIDERS:]
    (osb_ref, part_ref, gsb_ref), refs = refs[:3], refs[3:]
    cast_out, (ubuf, q_scr, kth, vh, tri_ref, carry_ref, acc_ref) = refs[:N_MIX_RIDERS], refs[N_MIX_RIDERS:]
    _cast_refs(cast_in, cast_out)
    t = SB_T
    tq = MIX_TM
    w2 = HEADS_PER_PAIR * t
    tile = pl.program_id(1)
    ja = 2 * tile

    @pl.when(tile == 0)
    def _():
        ubuf[0:SUBLANES, :] = jnp.zeros((SUBLANES, CONV_DIM), F32)
        for p in range(SB_NPAIR):
            kth[p, 0:SB_PAD] = jnp.zeros((SB_PAD, LANES, t), BF16)
            vh[p, 0:SB_PAD] = jnp.zeros((SB_PAD, t, LANES), BF16)

    h = _rms(x_ref[...], gmix_ref[...]).astype(BF16)

    def proj(off, width):
        return _dot(h, win_ref[:, off:off + width])

    q_scr[...] = proj(OFF_SQ, SB_DIM).astype(BF16)
    k_new = proj(OFF_SK, SB_DIM) * SB_K_SCALE
    v_new = proj(OFF_SV, SB_DIM).astype(BF16)
    for p in range(SB_NPAIR):
        for blk in range(SB_R):
            rows = slice(blk * t, (blk + 1) * t)
            kth[p, ja + SB_PAD + blk] = k_new[rows, _lanes(p)].T.astype(BF16)
            vh[p, ja + SB_PAD + blk] = v_new[rows, _lanes(p)]

    r = lax.broadcasted_iota(jnp.int32, (w2, w2), 0) % t
    c = lax.broadcasted_iota(jnp.int32, (w2, w2), 1)
    tri_ref[...] = jnp.where((c >= t) | (r >= c), -1.0, 0.0).astype(BF16)

    row = lax.broadcasted_iota(jnp.int32, (t, w2), 0)
    col = lax.broadcasted_iota(jnp.int32, (t, w2), 1) % t
    past = col < row

    feat_row = lax.broadcasted_iota(jnp.int32, (LANES, t), 0)
    feat_lane = lax.broadcasted_iota(jnp.int32, (t, LANES), 1)
    head_rows = [jnp.where(feat_row // SB_HEAD_DIM == hh, 1.0, 0.0).astype(BF16)
                 for hh in range(HEADS_PER_PAIR)]
    head_lanes = [jnp.where(feat_lane // SB_HEAD_DIM == hh, 1.0, 0.0).astype(BF16)
                  for hh in range(HEADS_PER_PAIR)]

    def mask_diag(x):
        head = jnp.where(past, x[:t], 0.0)
        return head if x.shape[0] == t else jnp.concatenate([head, x[t:]], axis=0)

    def put(full, r0, r1, new):
        parts = ([full[:r0]] if r0 > 0 else []) + [new] + ([full[r1:]] if r1 < full.shape[0] else [])
        return new if len(parts) == 1 else jnp.concatenate(parts, axis=0)

    def tiles_stages(tiles, carry, acc):
        carry, acc = dict(carry), dict(acc)
        zs, lhs = [], []
        for p, r0, r1, j, diag in tiles:
            kt = kth[p, j + SB_PAD]
            kt2 = jnp.concatenate([kt * m for m in head_rows], axis=1)
            zs.append(_dot(q_scr[r0:r1, _lanes(p)], kt2))
        yield None
        for z, (p, r0, r1, j, diag) in zip(zs, tiles):
            sp = jnp.maximum(z, jnp.log(1.0 + jnp.exp2(jnp.minimum(z, SB_Z_MAX))) * LOG2E)
            if diag:
                sp = mask_diag(sp)
            hi = sp.astype(BF16)
            lo = (sp - hi.astype(F32)).astype(BF16)
            lhs.append([jnp.concatenate([hi[:, hh * t:(hh + 1) * t], lo[:, hh * t:(hh + 1) * t]], axis=1)
                        for hh in range(HEADS_PER_PAIR)])
        res = [[_dot(x, tri_ref[...]) for x in xs] for xs in lhs]
        yield None
        avs = []
        for z, rs, (p, r0, r1, j, diag) in zip(zs, res, tiles):
            below = jnp.concatenate([x[:, :t] for x in rs], axis=1)
            total = jnp.concatenate([x[:, t:] for x in rs], axis=1)
            a = jnp.exp2(z + below + carry[p][r0:r1])
            if diag:
                a = mask_diag(a)
            avs.append(a.astype(BF16))
            carry[p] = put(carry[p], r0, r1, carry[p][r0:r1] + total)
        for a, (p, r0, r1, j, diag) in zip(avs, tiles):
            vb = vh[p, j + SB_PAD]
            v2 = jnp.concatenate([vb * m for m in head_lanes], axis=0)
            acc[p] = put(acc[p], r0, r1, acc[p][r0:r1] + _dot(a, v2))
        yield carry, acc

    early = SB_EARLY
    steps = [(t, tq, ja + 1, True), (0, tq, ja, True), (0, t + early, ja - 1, False), (0, early, ja - 2, False)]
    groups = [range(g, g + SB_GROUP) for g in range(0, SB_NPAIR, SB_GROUP)]
    sweeps = [tiles_stages([(p,) + st for st in steps for p in grp],
                           {p: jnp.zeros((tq, w2), F32) for p in grp},
                           {p: jnp.zeros((tq, LANES), F32) for p in grp}) for grp in groups]

    def token_local():
        u = proj(OFF_CC, CONV_DIM) * proj(OFF_CH, CONV_DIM)
        ubuf[SUBLANES:SUBLANES + tq, :] = u
        cw = convw_ref[...]
        conv = (cw[0:1, :] * ubuf[SUBLANES - 2:SUBLANES - 2 + tq, :]
                + cw[1:2, :] * ubuf[SUBLANES - 1:SUBLANES - 1 + tq, :]
                + cw[2:3, :] * u)
        ubuf[0:SUBLANES, :] = ubuf[tq:tq + SUBLANES, :]
        yield
        y_conv = _dot((proj(OFF_CB, CONV_DIM) * conv).astype(BF16), wco_ref[...])
        yield
        xq = proj(OFF_XQ, X_DIM)
        heads = []
        for hh in range(X_HEADS):
            lo = hh * X_HEAD_DIM
            qn = _rms(xq[:, lo:lo + X_HEAD_DIM], qg_ref[...]).astype(BF16)
            s = _dot(qn, kt_ref[lo:lo + X_HEAD_DIM, :]) * (X_HEAD_DIM ** -0.5)
            pr = jnp.exp(s - jnp.max(s, axis=-1, keepdims=True))
            l = jnp.sum(pr, axis=-1, keepdims=True)
            heads.append(_dot(pr.astype(BF16), v_ref[:, lo:lo + X_HEAD_DIM]) / l)
            if hh % 2 == 1:
                yield
        y_x = _dot(jnp.concatenate(heads, axis=-1).astype(BF16), wxo_ref[...])
        yield
        g_conv = jax.nn.sigmoid(proj(OFF_GATE, D_MODEL))
        yield
        g_x = jax.nn.sigmoid(proj(OFF_GATE + 2 * D_MODEL, D_MODEL))
        part_ref[...] = (g_conv * y_conv + g_x * y_x).astype(part_ref.dtype)
        yield
        gsb_ref[...] = jax.nn.sigmoid(proj(OFF_GATE + D_MODEL, D_MODEL)).astype(gsb_ref.dtype)

    local = token_local()
    pieces_per_stage = SB_GROUP // 2
    state = []
    for sweep in sweeps:
        next(sweep)
        for _ in range(pieces_per_stage):
            next(local, None)
        next(sweep)
        for _ in range(pieces_per_stage):
            next(local, None)
        state.append(next(sweep))
    for _ in local:
        pass

    carry = {p: c for cs, _ in state for p, c in cs.items()}
    acc = {p: a for _, accs in state for p, a in accs.items()}
    for p in range(SB_NPAIR):
        carry_ref[p] = carry[p]
        acc_ref[p] = acc[p]
        osb_ref[:, _lanes(p)] = acc[p].astype(osb_ref.dtype)

    def alive(rows):
        return functools.reduce(jnp.maximum, [jnp.max(x) for x in rows]) >= SB_DEAD

    rest = ((0, early, ja - 3), (early, t + early, ja - 2), (t + early, tq, ja - 1))
    alive0 = [alive([carry[p][r0:r1] for p in range(SB_NPAIR)]) for r0, r1, _ in rest]
    for (r0, r1, j_next), go in zip(rest, alive0):
        def k_cond(st):
            return jnp.logical_and(st[0] >= 0, st[1])

        def k_body(st, r0=r0, r1=r1):
            *_, (cnew, anew) = tiles_stages([(p, r0, r1, st[0], False) for p in range(SB_NPAIR)],
                                            {p: carry_ref[p] for p in range(SB_NPAIR)},
                                            {p: acc_ref[p] for p in range(SB_NPAIR)})
            for p in range(SB_NPAIR):
                carry_ref[p] = cnew[p]
                acc_ref[p] = anew[p]
                osb_ref[r0:r1, _lanes(p)] = anew[p][r0:r1].astype(osb_ref.dtype)
            return st[0] - 1, alive([cnew[p][r0:r1] for p in range(SB_NPAIR)])

        lax.while_loop(k_cond, k_body, (j_next, go))


def _mixer(x, kt, v, g_mix, q_norm_g, conv_w, w_in, w_conv_out, w_x_out, riders):
    b, s, d = x.shape
    m = v.shape[1]
    tm = MIX_TM
    nt = s // tm
    nblk = s // SB_T
    w2 = HEADS_PER_PAIR * SB_T
    assert len(riders) == N_MIX_RIDERS
    r_in, r_in_specs, r_out_specs, r_out_shapes = _cast_rider(riders, b * nt, lambda i, j: i * nt + j)
    tok = lambda width: pl.BlockSpec((None, tm, width), lambda i, j: (i, j, 0))
    outs = pl.pallas_call(
        _mixer_kernel,
        grid=(b, s // tm),
        in_specs=[
            tok(d),
            pl.BlockSpec((None, X_DIM, m), lambda i, j: (i, 0, 0), pipeline_mode=pl.Buffered(1)),
            pl.BlockSpec((None, m, X_DIM), lambda i, j: (i, 0, 0), pipeline_mode=pl.Buffered(1)),
            _resident((1, d)),
            _resident((1, X_HEAD_DIM)),
            _resident((CONV_WIDTH, CONV_DIM)),
            _resident((d, IN_COLS)),
            _resident((CONV_DIM, d)),
            _resident((X_DIM, d)),
        ] + r_in_specs,
        out_specs=[tok(SB_DIM), tok(d), tok(d)] + r_out_specs,
        out_shape=[
            jax.ShapeDtypeStruct((b, s, SB_DIM), BF16),
            jax.ShapeDtypeStruct((b, s, d), BF16),
            jax.ShapeDtypeStruct((b, s, d), BF16),
        ] + r_out_shapes,
        scratch_shapes=[
            pltpu.VMEM((tm + SUBLANES, CONV_DIM), F32),
            pltpu.VMEM((tm, SB_DIM), BF16),
            pltpu.VMEM((SB_NPAIR, nblk + SB_PAD, LANES, SB_T), BF16),
            pltpu.VMEM((SB_NPAIR, nblk + SB_PAD, SB_T, LANES), BF16),
            pltpu.VMEM((w2, w2), BF16),
            pltpu.VMEM((SB_NPAIR, tm, w2), F32),
            pltpu.VMEM((SB_NPAIR, tm, LANES), F32),
        ],
        compiler_params=pltpu.CompilerParams(
            dimension_semantics=("arbitrary", "arbitrary"), vmem_limit_bytes=VMEM_LIMIT),
        name="mixer",
    )(x, kt, v, g_mix, q_norm_g, conv_w, w_in, w_conv_out, w_x_out, *r_in)
    return outs[0], outs[1], outs[2], [o.reshape(w.shape) for o, w in zip(outs[3:], riders)]


def _post_kernel(x_ref, osb_ref, part_ref, gsb_ref, gmlp_ref, wsb_ref, wout_ref, wup_ref, wdn_ref,
                 out_ref):
    y_sb = _dot(osb_ref[...], wsb_ref[...])
    merged = part_ref[...] + gsb_ref[...] * y_sb
    x1 = x_ref[...] + _dot(merged.astype(BF16), wout_ref[...])
    h2 = _rms(x1, gmlp_ref[...]).astype(BF16)
    acc = x1
    for cidx in range(D_FF // FF_CHUNK):
        lo = cidx * FF_CHUNK
        up = jnp.maximum(_dot(h2, wup_ref[:, lo:lo + FF_CHUNK]), 0.0)
        acc = acc + _dot((up * up).astype(BF16), wdn_ref[lo:lo + FF_CHUNK, :])
    out_ref[...] = acc


def _post(x, o_sb, part, g_sb, g_mlp, w_sb_out, w_out, w_up, w_down):
    b, s, d = x.shape
    tm = POST_TM
    tok = lambda width: pl.BlockSpec((None, tm, width), lambda i, j: (i, j, 0))
    return pl.pallas_call(
        _post_kernel,
        grid=(b, s // tm),
        in_specs=[
            tok(d), tok(SB_DIM), tok(d), tok(d),
            _resident((1, d)),
            _resident((SB_DIM, d)),
            _resident((d, d)),
            _resident((d, D_FF)),
            _resident((D_FF, d)),
        ],
        out_specs=tok(d),
        out_shape=jax.ShapeDtypeStruct((b, s, d), F32),
        compiler_params=pltpu.CompilerParams(
            dimension_semantics=("arbitrary", "arbitrary"), vmem_limit_bytes=VMEM_LIMIT),
        name="post",
    )(x, o_sb, part, g_sb, g_mlp, w_sb_out, w_out, w_up, w_down)


def kernel(x, mem, g_mix, g_mem, w_in, conv_w, w_conv_out, w_sb_out, q_norm_g, k_norm_g,
           w_mem_kv, w_x_out, w_out, g_mlp, w_up, w_down):
    depth = w_in.shape[0]
    for l in range(depth):
        row = lambda g: g[l][None, :]
        kt, v, mixer_w = _mem_kv(mem, row(g_mem), row(k_norm_g), w_mem_kv[l],
                                 (w_in[l], w_conv_out[l], w_x_out[l]))
        o_sb, part, g_sb, post_w = _mixer(
            x, kt, v, row(g_mix), row(q_norm_g), conv_w[l], *mixer_w,
            (w_sb_out[l], w_out[l], w_up[l], w_down[l]))
        x = _post(x, o_sb, part, g_sb, row(g_mlp), *post_w)
    return x
```
